```python
import math
import jax
import jax.numpy as jnp
from jax import lax
import numpy as np

D_MODEL = 4096
BATCH = 4
SEQ = 2048
DEPTH = 4
DEC_BATCH = 8
DEC_SEQ = 2048
PAST_LEN = 128

N_MIXERS = 3
DA_HEADS = 16
DA_DH = D_MODEL // (2 * DA_HEADS)
QBLK = 128
SG_CHUNK = 128
SG_WIDTH = 2 * D_MODEL
SG_GROUPS = 16
SG_GW = SG_WIDTH // SG_GROUPS
RG_WIDTH = D_MODEL
RG_BLOCKS = 16
RG_BW = RG_WIDTH // RG_BLOCKS
RG_C = 8.0
CONV_W = 4
CONV_LEFT = 2
MEM_LEN = 256
CA_HEADS = 4
CA_DH = D_MODEL // CA_HEADS
N_EXPERTS = 32
N_GROUPS = 8
EXPERTS_PER_GROUP = N_EXPERTS // N_GROUPS
TOP_K = 2
D_FF_EXPERT = 1024
MOE_BLK = 128
DEEPNORM_ALPHA = (2 * DEPTH) ** 0.25
DEEPNORM_BETA = (8 * DEPTH) ** -0.25
LN_EPS = 1e-5

kernel_name = "hybrid_diffattn_sgmlp_rglru_moe_encoder"


def layer_norm(x, g, b):
    xf = x.astype(jnp.float32)
    mu = jnp.mean(xf, axis=-1, keepdims=True)
    xc = xf - mu
    var = jnp.mean(xc * xc, axis=-1, keepdims=True)
    return (xc * lax.rsqrt(var + LN_EPS) * g.astype(jnp.float32) + b.astype(jnp.float32)).astype(x.dtype)


def rms_norm(x, g):
    xf = x.astype(jnp.float32)
    ms = jnp.mean(xf * xf, axis=-1, keepdims=True)
    return (xf * lax.rsqrt(ms + LN_EPS) * g.astype(jnp.float32)).astype(x.dtype)


def alibi_slopes(n_heads):
    return jnp.asarray(2.0 ** (-8.0 * np.arange(1, n_heads + 1) / n_heads), dtype=jnp.float32)


def diff_attention(x, w_qkv, w_o, lam, subln_g, layer_idx):
    B, S, _ = x.shape
    q, k, v = jnp.split(x @ w_qkv, 3, axis=-1)
    q = q.reshape(B, S, DA_HEADS, 2, DA_DH)
    k = k.reshape(B, S, DA_HEADS, 2, DA_DH)
    v = v.reshape(B, S, DA_HEADS, 2 * DA_DH)
    lam_init = 0.8 - 0.6 * math.exp(-0.3 * layer_idx)
    lf = lam.astype(jnp.float32)
    lam_full = jnp.exp(jnp.sum(lf[0] * lf[1])) - jnp.exp(jnp.sum(lf[2] * lf[3])) + lam_init
    slopes = alibi_slopes(DA_HEADS)
    kpos = jnp.arange(S, dtype=jnp.float32)
    nq = S // QBLK
    qb = q.reshape(B, nq, QBLK, DA_HEADS, 2, DA_DH).transpose(1, 0, 2, 3, 4, 5)
    qpos = kpos.reshape(nq, QBLK)
    scale = DA_DH ** -0.5

    def block(args):
        qi, pi = args
        s = jnp.einsum('bqhcd,bkhcd->bhcqk', qi, k, preferred_element_type=jnp.float32) * scale
        bias = -slopes[:, None, None] * jnp.abs(pi[:, None] - kpos[None, :])
        p = jax.nn.softmax(s + bias[None, :, None], axis=-1)
        a = p[:, :, 0] - lam_full * p[:, :, 1]
        return jnp.einsum('bhqk,bkhe->bqhe', a.astype(v.dtype), v)

    o = lax.map(block, (qb, qpos))
    o = o.transpose(1, 0, 2, 3, 4).reshape(B, S, DA_HEADS, 2 * DA_DH)
    o = rms_norm(o, subln_g) * (1.0 - lam_init)
    return o.reshape(B, S, D_MODEL) @ w_o


def spatial_gating_mlp(x, w_in, b_in, ln_g, ln_b, w_s, b_s, w_o):
    B, S, _ = x.shape
    z = jax.nn.gelu(x @ w_in + b_in)
    u, v = jnp.split(z, 2, axis=-1)
    v = layer_norm(v, ln_g, ln_b)
    n = S // SG_CHUNK
    v = v.reshape(B, n, SG_CHUNK, SG_GROUPS, SG_GW)
    v = jnp.einsum('gts,bnsgc->bntgc', w_s, v) + b_s.T[:, :, None]
    return (u * v.reshape(B, S, SG_WIDTH)) @ w_o


def _linear_combine(left, right):
    a_l, b_l = left
    a_r, b_r = right
    return a_l * a_r, a_r * b_l + b_r


def rglru_block(x, w_in, conv_w, conv_b, w_a, b_a, w_i, b_i, lam, w_o):
    B, S, _ = x.shape
    xr, g = jnp.split(x @ w_in, 2, axis=-1)
    gate = jax.nn.gelu(g)
    xp = jnp.pad(xr, ((0, 0), (CONV_LEFT, CONV_W - 1 - CONV_LEFT), (0, 0)))
    xc = sum(xp[:, j:j + S] * conv_w[j] for j in range(CONV_W)) + conv_b
    xblk = xc.reshape(B, S, RG_BLOCKS, RG_BW)
    xcf = xc.astype(jnp.float32)
    h = jnp.zeros((B, S, RG_WIDTH), jnp.float32)
    for d, rev in ((0, False), (1, True)):
        r = jax.nn.sigmoid(jnp.einsum('bsnc,ncd->bsnd', xblk, w_a[d]).reshape(B, S, RG_WIDTH).astype(jnp.float32)
                           + b_a[d].astype(jnp.float32))
        i = jax.nn.sigmoid(jnp.einsum('bsnc,ncd->bsnd', xblk, w_i[d]).reshape(B, S, RG_WIDTH).astype(jnp.float32)
                           + b_i[d].astype(jnp.float32))
        log_a = -RG_C * r * jax.nn.softplus(-lam[d].astype(jnp.float32))
        a = jnp.exp(log_a)
        bt = jnp.sqrt(jnp.maximum(-jnp.expm1(2.0 * log_a), 0.0)) * (i * xcf)
        _, hd = lax.associative_scan(_linear_combine, (a, bt), axis=1, reverse=rev)
        h = h + hd
    return (h.astype(x.dtype) * gate) @ w_o


def memory_attention(x, mem, w_q, w_kv, w_o):
    B, S, _ = x.shape
    M = mem.shape[1]
    q = (x @ w_q).reshape(B, S, CA_HEADS, CA_DH)
    k, v = jnp.split(mem @ w_kv, 2, axis=-1)
    k = k.reshape(B, M, CA_HEADS, CA_DH)
    v = v.reshape(B, M, CA_HEADS, CA_DH)
    s = jnp.einsum('bqhd,bkhd->bhqk', q, k, preferred_element_type=jnp.float32) * (CA_DH ** -0.5)
    p = jax.nn.softmax(s, axis=-1)
    o = jnp.einsum('bhqk,bkhd->bqhd', p.astype(v.dtype), v).reshape(B, S, D_MODEL)
    return o @ w_o


def route(xf, router_w, router_b):
    T = xf.shape[0]
    s = jax.nn.sigmoid((xf @ router_w).astype(jnp.float32))
    biased = (s + router_b.astype(jnp.float32)).reshape(T, N_GROUPS, EXPERTS_PER_GROUP)
    gscore = jnp.sum(lax.top_k(biased, TOP_K)[0], axis=-1)
    gsel = jnp.argmax(gscore, axis=-1)
    mask = jnp.arange(N_GROUPS)[None, :] == gsel[:, None]
    masked = jnp.where(mask[:, :, None], biased, -jnp.inf).reshape(T, N_EXPERTS)
    _, idx = lax.top_k(masked, TOP_K)
    w = jnp.take_along_axis(s, idx, axis=-1)
    return idx, w / jnp.sum(w, axis=-1, keepdims=True)


def routed_experts(x, router_w, router_b, w_gu, w_dn):
    B, S, D = x.shape
    T = B * S
    xf = x.reshape(T, D)
    idx, gate = route(xf, router_w, router_b)
    tk = T * TOP_K
    flat_e = idx.reshape(-1).astype(jnp.int32)
    flat_w = gate.reshape(-1)
    order = jnp.argsort(flat_e)
    e_s = flat_e[order]
    tok_s = (order // TOP_K).astype(jnp.int32)
    w_s = flat_w[order]
    counts = jnp.bincount(flat_e, length=N_EXPERTS).astype(jnp.int32)
    padded = (counts + MOE_BLK - 1) // MOE_BLK * MOE_BLK
    pad_end = jnp.cumsum(padded)
    pad_start = pad_end - padded
    start = jnp.cumsum(counts) - counts
    dest = pad_start[e_s] + jnp.arange(tk, dtype=jnp.int32) - start[e_s]
    n_blk = -(-tk // MOE_BLK) + N_EXPERTS
    rows = n_blk * MOE_BLK
    buf_tok = jnp.zeros((rows,), jnp.int32).at[dest].set(tok_s)
    buf_w = jnp.zeros((rows,), jnp.float32).at[dest].set(w_s)
    blk_e = jnp.minimum(jnp.searchsorted(pad_end, jnp.arange(n_blk, dtype=jnp.int32) * MOE_BLK, side='right'),
                        N_EXPERTS - 1)
    xb = xf[buf_tok].reshape(n_blk, MOE_BLK, D)

    def expert_block(args):
        xi, e = args
        g, u = jnp.split(xi @ w_gu[e], 2, axis=-1)
        return (jax.nn.silu(g) * u) @ w_dn[e]

    yb = lax.map(expert_block, (xb, blk_e)).reshape(rows, D)
    y = jnp.zeros((T, D), x.dtype).at[buf_tok].add(yb * buf_w[:, None].astype(x.dtype))
    return y.reshape(B, S, D)


def setup_inputs(seed: int = 0) -> dict:
    key = jax.random.key(seed)
    ks = iter(jax.random.split(key, 48))
    f32 = jnp.float32

    def nrm(shape, scale):
        return jax.random.normal(next(ks), shape, f32) * scale

    D = D_MODEL
    n_a, n_b, n_c = (DEPTH + 2) // 3, (DEPTH + 1) // 3, DEPTH // 3
    beta = DEEPNORM_BETA
    a0 = jax.random.uniform(next(ks), (n_c, 2, RG_WIDTH), f32, 0.9, 0.999)
    p = a0 ** (1.0 / RG_C)
    rg_lambda = jnp.log(p) - jnp.log1p(-p)
    return {
        "x_prompt": nrm((BATCH, SEQ, D), 1.0),
        "x_sample": nrm((DEC_BATCH, DEC_SEQ, D), 1.0),
        "mem_prompt": nrm((BATCH, MEM_LEN, D), 1.0),
        "mem_sample": nrm((DEC_BATCH, MEM_LEN, D), 1.0),
        "da_w_qkv": nrm((n_a, D, 3 * D), D ** -0.5),
        "da_lam": nrm((n_a, 4, DA_DH), 0.1),
        "da_subln_g": 1.0 + nrm((n_a, 2 * DA_DH), 0.02),
        "da_w_o": nrm((n_a, D, D), D ** -0.5 * beta),
        "sg_w_in": nrm((n_b, D, 2 * SG_WIDTH), D ** -0.5),
        "sg_b_in": nrm((n_b, 2 * SG_WIDTH), 0.02),
        "sg_ln_g": 1.0 + nrm((n_b, SG_WIDTH), 0.02),
        "sg_ln_b": nrm((n_b, SG_WIDTH), 0.02),
        "sg_w_s": nrm((n_b, SG_GROUPS, SG_CHUNK, SG_CHUNK), SG_CHUNK ** -0.5),
        "sg_b_s": 1.0 + nrm((n_b, SG_GROUPS, SG_CHUNK), 0.02),
        "sg_w_o": nrm((n_b, SG_WIDTH, D), SG_WIDTH ** -0.5 * beta),
        "rg_w_in": nrm((n_c, D, 2 * RG_WIDTH), D ** -0.5),
        "rg_conv_w": nrm((n_c, CONV_W, RG_WIDTH), CONV_W ** -0.5),
        "rg_conv_b": nrm((n_c, RG_WIDTH), 0.02),
        "rg_w_a": nrm((n_c, 2, RG_BLOCKS, RG_BW, RG_BW), RG_BW ** -0.5),
        "rg_b_a": nrm((n_c, 2, RG_WIDTH), 0.02),
        "rg_w_i": nrm((n_c, 2, RG_BLOCKS, RG_BW, RG_BW), RG_BW ** -0.5),
        "rg_b_i": nrm((n_c, 2, RG_WIDTH), 0.02),
        "rg_lambda": rg_lambda,
        "rg_w_o": nrm((n_c, RG_WIDTH, D), RG_WIDTH ** -0.5 * beta),
        "ca_w_q": nrm((DEPTH, D, D), D ** -0.5),
        "ca_w_kv": nrm((DEPTH, D, 2 * D), D ** -0.5),
        "ca_w_o": nrm((DEPTH, D, D), D ** -0.5 * beta),
        "router_w": nrm((D, N_EXPERTS), D ** -0.5),
        "router_b": nrm((N_EXPERTS,), 0.01),
        "moe_w_gu": nrm((DEPTH, N_EXPERTS, D, 2 * D_FF_EXPERT), D ** -0.5),
        "moe_w_dn": nrm((DEPTH, N_EXPERTS, D_FF_EXPERT, D), D_FF_EXPERT ** -0.5 * beta),
        "ln_g": 1.0 + nrm((DEPTH, 3, D), 0.02),
        "ln_b": nrm((DEPTH, 3, D), 0.02),
    }


def reference(x_prompt, x_sample, mem_prompt, mem_sample,
              da_w_qkv, da_lam, da_subln_g, da_w_o,
              sg_w_in, sg_b_in, sg_ln_g, sg_ln_b, sg_w_s, sg_b_s, sg_w_o,
              rg_w_in, rg_conv_w, rg_conv_b, rg_w_a, rg_b_a, rg_w_i, rg_b_i, rg_lambda, rg_w_o,
              ca_w_q, ca_w_kv, ca_w_o,
              router_w, router_b, moe_w_gu, moe_w_dn,
              ln_g, ln_b):
    def trunk(x, mem):
        for i in range(DEPTH):
            kind, slot = i % N_MIXERS, i // N_MIXERS
            if kind == 0:
                h = diff_attention(x, da_w_qkv[slot], da_w_o[slot], da_lam[slot], da_subln_g[slot], i)
            elif kind == 1:
                h = spatial_gating_mlp(x, sg_w_in[slot], sg_b_in[slot], sg_ln_g[slot], sg_ln_b[slot],
                                       sg_w_s[slot], sg_b_s[slot], sg_w_o[slot])
            else:
                h = rglru_block(x, rg_w_in[slot], rg_conv_w[slot], rg_conv_b[slot], rg_w_a[slot], rg_b_a[slot],
                                rg_w_i[slot], rg_b_i[slot], rg_lambda[slot], rg_w_o[slot])
            x = layer_norm(DEEPNORM_ALPHA * x + h, ln_g[i, 0], ln_b[i, 0])
            h = memory_attention(x, mem, ca_w_q[i], ca_w_kv[i], ca_w_o[i])
            x = layer_norm(DEEPNORM_ALPHA * x + h, ln_g[i, 1], ln_b[i, 1])
            h = routed_experts(x, router_w, router_b, moe_w_gu[i], moe_w_dn[i])
            x = layer_norm(DEEPNORM_ALPHA * x + h, ln_g[i, 2], ln_b[i, 2])
        return x

    y_prompt = trunk(x_prompt, mem_prompt)
    y_sample = trunk(x_sample, mem_sample)
    return (y_prompt, y_sample)
```

```python
import functools
import math

import jax
import jax.numpy as jnp
from jax import lax
from jax.experimental import pallas as pl
from jax.experimental.pallas import tpu as pltpu

DEPTH = 4
N_MIXERS = 3
DA_HEADS = 16
SG_CHUNK = 128
SG_GROUPS = 16
RG_BLOCKS = 16
RG_C = 8.0
CA_HEADS = 4
N_EXPERTS = 32
N_GROUPS = 8
LN_EPS = 1e-5
DEEPNORM_ALPHA = (2 * DEPTH) ** 0.25

V7X_VMEM_BYTES = 64 * 1024 * 1024
VMEM_CAP = V7X_VMEM_BYTES - 6 * 1024 * 1024
SUBLANES = 8
LANES = 128

MOE_BM = 512
MOE_TF = 256
MOE_TG = 256

BF16 = jnp.bfloat16
F32 = jnp.float32


def _tile(dim, pref, align):
    if dim <= pref:
        return dim
    t = (pref // align) * align
    while t > align and dim % t:
        t -= align
    assert dim % t == 0, (dim, pref, align)
    return t


def _params(sem, vmem_bytes):
    return pltpu.CompilerParams(dimension_semantics=sem,
                                vmem_limit_bytes=int(min(max(vmem_bytes, 16 * 2**20), VMEM_CAP)))


def _mm_kernel(*refs, act, has_bias):
    if has_bias:
        x_ref, w_ref, b_ref, o_ref, wbf_ref = refs
    else:
        x_ref, w_ref, o_ref, wbf_ref = refs

    @pl.when(pl.program_id(1) == 0)
    def _():
        wbf_ref[...] = w_ref[...].astype(BF16)

    acc = jnp.dot(x_ref[...].astype(BF16), wbf_ref[...], preferred_element_type=F32)
    if has_bias:
        acc = acc + b_ref[...]
    if act == "gelu":
        acc = jax.nn.gelu(acc)
    o_ref[...] = acc.astype(o_ref.dtype)


def _mm(x, w, bias=None, act=None, out_dtype=F32, name="mm"):
    M, K = x.shape
    K2, N = w.shape
    assert K == K2
    xb = x.dtype.itemsize
    ob = jnp.dtype(out_dtype).itemsize
    tm = _tile(M, max(256, (8 * 2**20) // (K * xb)), 256 if M % 256 == 0 else SUBLANES)
    tn = _tile(N, max(256, (8 * 2**20) // (K * 4)), 256 if N % 256 == 0 else LANES)
    grid = (N // tn, M // tm)
    in_specs = [pl.BlockSpec((tm, K), lambda j, i: (i, 0)),
                pl.BlockSpec((K, tn), lambda j, i: (0, j))]
    args = [x, w]
    if bias is not None:
        in_specs.append(pl.BlockSpec((1, tn), lambda j, i: (0, j)))
        args.append(bias.reshape(1, N).astype(F32))
    vmem = 2 * tm * K * xb + 2 * K * tn * 4 + K * tn * 2 + 2 * tm * tn * ob + 2 * tm * tn * 4 + 4 * 2**20
    return pl.pallas_call(
        functools.partial(_mm_kernel, act=act, has_bias=bias is not None),
        grid=grid,
        in_specs=in_specs,
        out_specs=pl.BlockSpec((tm, tn), lambda j, i: (i, j)),
        out_shape=jax.ShapeDtypeStruct((M, N), out_dtype),
        scratch_shapes=[pltpu.VMEM((K, tn), BF16)],
        compiler_params=_params(("parallel", "arbitrary"), vmem),
        name=name,
    )(*args)


def _layer_norm_rows(y, g, b):
    mu = jnp.mean(y, axis=-1, keepdims=True)
    yc = y - mu
    var = jnp.mean(yc * yc, axis=-1, keepdims=True)
    return yc * lax.rsqrt(var + LN_EPS) * g + b


def _add_ln_kernel(x_ref, h_ref, g_ref, b_ref, o_ref, ob_ref):
    y = DEEPNORM_ALPHA * x_ref[...] + h_ref[...].astype(F32)
    out = _layer_norm_rows(y, g_ref[...], b_ref[...])
    o_ref[...] = out
    ob_ref[...] = out.astype(BF16)


def _add_ln(x, h, g, b):
    T, D = x.shape
    tm = _tile(T, 256, SUBLANES)
    row = pl.BlockSpec((tm, D), lambda i: (i, 0))
    vec = pl.BlockSpec((1, D), lambda i: (0, 0))
    vmem = 2 * tm * D * (4 + h.dtype.itemsize + 4 + 2) + 4 * tm * D * 4
    return pl.pallas_call(
        _add_ln_kernel,
        grid=(T // tm,),
        in_specs=[row, row, vec, vec],
        out_specs=[row, row],
        out_shape=[jax.ShapeDtypeStruct((T, D), F32), jax.ShapeDtypeStruct((T, D), BF16)],
        compiler_params=_params(("parallel",), vmem),
        name="add_ln",
    )(x, h, g.reshape(1, D), b.reshape(1, D))


def _da_kernel(slope_ref, lam_ref, q_ref, k_ref, v_ref, g_ref, o_ref, *, tq, lam_init):
    h = pl.program_id(1)
    qi = pl.program_id(2)
    S = k_ref.shape[0]
    dh = q_ref.shape[1] // 2
    neg_slope = -slope_ref[h]
    lam = lam_ref[0]
    q = q_ref[...]
    k = k_ref[...]
    qpos = qi * tq + lax.broadcasted_iota(jnp.int32, (tq, S), 0)
    kpos = lax.broadcasted_iota(jnp.int32, (tq, S), 1)
    bias = jnp.abs(qpos - kpos).astype(F32) * neg_slope
    scale = dh ** -0.5

    def exp_scores(c):
        s = lax.dot_general(q[:, c * dh:(c + 1) * dh], k[:, c * dh:(c + 1) * dh],
                            (((1,), (1,)), ((), ())), preferred_element_type=F32)
        s = s * scale + bias
        m = jnp.max(s, axis=-1, keepdims=True)
        e = jnp.exp(s - m)
        return e, jnp.sum(e, axis=-1, keepdims=True)

    e1, l1 = exp_scores(0)
    e2, l2 = exp_scores(1)
    a = e1 * (1.0 / l1) - e2 * (lam / l2)
    o = jnp.dot(a.astype(BF16), v_ref[...], preferred_element_type=F32)
    ms = jnp.mean(o * o, axis=-1, keepdims=True)
    o = o * lax.rsqrt(ms + LN_EPS) * g_ref[...] * (1.0 - lam_init)
    o_ref[...] = o.astype(o_ref.dtype)


def _diff_attention(qkv, B, S, slopes, lam_full, subln_g, lam_init):
    T, D3 = qkv.shape
    D = D3 // 3
    H = DA_HEADS
    hw = D // H
    tq = _tile(S, 256, SUBLANES)
    nq = S // tq
    smem = pl.BlockSpec(memory_space=pltpu.SMEM)
    vmem = 8 * tq * S * 4 + 8 * S * hw * 2 + 4 * 2**20
    return pl.pallas_call(
        functools.partial(_da_kernel, tq=tq, lam_init=lam_init),
        grid=(B, H, nq),
        in_specs=[smem, smem,
                  pl.BlockSpec((tq, hw), lambda b, h, qi: (b * nq + qi, h)),
                  pl.BlockSpec((S, hw), lambda b, h, qi: (b, H + h)),
                  pl.BlockSpec((S, hw), lambda b, h, qi: (b, 2 * H + h)),
                  pl.BlockSpec((1, hw), lambda b, h, qi: (0, 0))],
        out_specs=pl.BlockSpec((tq, hw), lambda b, h, qi: (b * nq + qi, h)),
        out_shape=jax.ShapeDtypeStruct((T, D), BF16),
        compiler_params=_params(("parallel", "parallel", "parallel"), vmem),
        name="diff_attn",
    )(slopes, lam_full, qkv, qkv, qkv, subln_g.reshape(1, hw).astype(F32))


def _ca_kernel(q_ref, k_ref, v_ref, o_ref):
    dh = q_ref.shape[1]
    s = lax.dot_general(q_ref[...], k_ref[...], (((1,), (1,)), ((), ())),
                        preferred_element_type=F32) * (dh ** -0.5)
    m = jnp.max(s, axis=-1, keepdims=True)
    e = jnp.exp(s - m)
    p = e * (1.0 / jnp.sum(e, axis=-1, keepdims=True))
    o_ref[...] = jnp.dot(p.astype(BF16), v_ref[...], preferred_element_type=F32).astype(o_ref.dtype)


def _memory_attention_core(q, kv, B, S, M):
    T, D = q.shape
    H = CA_HEADS
    dh = D // H
    tq = _tile(S, 512, SUBLANES)
    nq = S // tq
    vmem = 4 * tq * dh * 2 * 2 + 4 * M * dh * 2 + 4 * tq * M * 4 + 2 * tq * dh * 4 + 4 * 2**20
    return pl.pallas_call(
        _ca_kernel,
        grid=(B, H, nq),
        in_specs=[pl.BlockSpec((tq, dh), lambda b, h, qi: (b * nq + qi, h)),
                  pl.BlockSpec((M, dh), lambda b, h, qi: (b, h)),
                  pl.BlockSpec((M, dh), lambda b, h, qi: (b, H + h))],
        out_specs=pl.BlockSpec((tq, dh), lambda b, h, qi: (b * nq + qi, h)),
        out_shape=jax.ShapeDtypeStruct((T, D), BF16),
        compiler_params=_params(("parallel", "parallel", "parallel"), vmem),
        name="mem_attn",
    )(q, kv, kv)


def _sg_kernel(u_ref, v_ref, g_ref, b_ref, ws_ref, bs_ref, o_ref, *, n_chunks):
    W = v_ref.shape[1]
    G = ws_ref.shape[0]
    gw = W // G
    vn = _layer_norm_rows(v_ref[...].astype(F32), g_ref[...], b_ref[...]).astype(BF16)
    for c in range(n_chunks):
        rows = slice(c * SG_CHUNK, (c + 1) * SG_CHUNK)
        for g in range(G):
            cols = slice(g * gw, (g + 1) * gw)
            y = jnp.dot(ws_ref[g], vn[rows, cols], preferred_element_type=F32) + bs_ref[:, g:g + 1]
            o_ref[rows, cols] = (u_ref[rows, cols].astype(F32) * y).astype(o_ref.dtype)


def _spatial_gate(z, ln_g, ln_b, w_s, b_s):
    T, W2 = z.shape
    W = W2 // 2
    n_chunks = 2 if T % (2 * SG_CHUNK) == 0 else 1
    R = n_chunks * SG_CHUNK
    G = w_s.shape[0]
    zb = z.dtype.itemsize
    vmem = 4 * R * W * zb + 2 * R * W * 2 + 4 * R * W * 4 + 4 * 2**20
    return pl.pallas_call(
        functools.partial(_sg_kernel, n_chunks=n_chunks),
        grid=(T // R,),
        in_specs=[pl.BlockSpec((R, W), lambda i: (i, 0)),
                  pl.BlockSpec((R, W), lambda i: (i, 1)),
                  pl.BlockSpec((1, W), lambda i: (0, 0)),
                  pl.BlockSpec((1, W), lambda i: (0, 0)),
                  pl.BlockSpec((G, SG_CHUNK, SG_CHUNK), lambda i: (0, 0, 0)),
                  pl.BlockSpec((SG_CHUNK, G), lambda i: (0, 0))],
        out_specs=pl.BlockSpec((R, W), lambda i: (i, 0)),
        out_shape=jax.ShapeDtypeStruct((T, W), BF16),
        compiler_params=_params(("parallel",), vmem),
        name="spatial_gate",
    )(z, z, ln_g.reshape(1, W), ln_b.reshape(1, W), w_s.astype(BF16), b_s.T.astype(F32))


def _rg_kernel(xr_ref, g_ref, cw_ref, cb_ref, w4_ref, b4_ref, c_ref, o_ref, a_s, b_s):
    S, C = xr_ref.shape
    xr = xr_ref[...].astype(F32)
    t = lax.broadcasted_iota(jnp.int32, (S, C), 0)
    x_m2 = jnp.where(t >= 2, pltpu.roll(xr, 2, 0), 0.0)
    x_m1 = jnp.where(t >= 1, pltpu.roll(xr, 1, 0), 0.0)
    x_p1 = jnp.where(t < S - 1, pltpu.roll(xr, S - 1, 0), 0.0)
    xc = x_m2 * cw_ref[0:1, :] + x_m1 * cw_ref[1:2, :] + xr * cw_ref[2:3, :] + x_p1 * cw_ref[3:4, :] + cb_ref[...]
    pre = jnp.dot(xc.astype(BF16), w4_ref[...], preferred_element_type=F32) + b4_ref[...]
    for d in range(2):
        r = jax.nn.sigmoid(pre[:, (2 * d) * C:(2 * d + 1) * C])
        i = jax.nn.sigmoid(pre[:, (2 * d + 1) * C:(2 * d + 2) * C])
        a = jnp.exp(c_ref[d:d + 1, :] * r)
        a_s[d] = a
        b_s[d] = jnp.sqrt(jnp.maximum(1.0 - a * a, 0.0)) * (i * xc)

    row = lax.broadcasted_iota(jnp.int32, (SUBLANES, C), 0)
    n_groups = S // SUBLANES

    def group_scan(A, Bv, reverse):
        for sh in (1, 2, 4):
            if reverse:
                ok = row < SUBLANES - sh
                rs = SUBLANES - sh
            else:
                ok = row >= sh
                rs = sh
            A_sh = jnp.where(ok, pltpu.roll(A, rs, 0), 1.0)
            B_sh = jnp.where(ok, pltpu.roll(Bv, rs, 0), 0.0)
            Bv = A * B_sh + Bv
            A = A * A_sh
        return A, Bv

    def body(gi, carry):
        hf, hb = carry
        r0 = pl.multiple_of(gi * SUBLANES, SUBLANES)
        A, Bv = group_scan(a_s[0, pl.ds(r0, SUBLANES), :], b_s[0, pl.ds(r0, SUBLANES), :], False)
        Hf = A * hf + Bv
        b_s[0, pl.ds(r0, SUBLANES), :] = Hf
        r1 = pl.multiple_of((n_groups - 1 - gi) * SUBLANES, SUBLANES)
        A, Bv = group_scan(a_s[1, pl.ds(r1, SUBLANES), :], b_s[1, pl.ds(r1, SUBLANES), :], True)
        Hb = A * hb + Bv
        b_s[1, pl.ds(r1, SUBLANES), :] = Hb
        return Hf[SUBLANES - 1:SUBLANES, :], Hb[0:1, :]

    zero = jnp.zeros((1, C), F32)
    lax.fori_loop(0, n_groups, body, (zero, zero), unroll=2)
    gate = jax.nn.gelu(g_ref[...].astype(F32))
    o_ref[...] = ((b_s[0] + b_s[1]) * gate).astype(o_ref.dtype)


def _rglru_core(xg, B, S, conv_w, conv_b, w_a, b_a, w_i, b_i, lam):
    T, R2 = xg.shape
    R = R2 // 2
    nb = RG_BLOCKS
    C = R // nb
    w4 = jnp.concatenate([w_a[0], w_i[0], w_a[1], w_i[1]], axis=-1).astype(BF16)
    b4 = jnp.stack([b_a[0].reshape(nb, C), b_i[0].reshape(nb, C), b_a[1].reshape(nb, C), b_i[1].reshape(nb, C)],
                   axis=1).reshape(nb, 1, 4 * C).astype(F32)
    coef = (-RG_C) * jax.nn.softplus(-lam.astype(F32))
    xb = xg.dtype.itemsize
    vmem = 4 * S * C * xb + 2 * S * C * 2 + 4 * S * C * 4 + 3 * S * 4 * C * 4 + 6 * S * C * 4 + 4 * 2**20
    return pl.pallas_call(
        _rg_kernel,
        grid=(B, nb),
        in_specs=[pl.BlockSpec((S, C), lambda b, n: (b, n)),
                  pl.BlockSpec((S, C), lambda b, n: (b, nb + n)),
                  pl.BlockSpec((4, C), lambda b, n: (0, n)),
                  pl.BlockSpec((1, C), lambda b, n: (0, n)),
                  pl.BlockSpec((None, C, 4 * C), lambda b, n: (n, 0, 0)),
                  pl.BlockSpec((None, 1, 4 * C), lambda b, n: (n, 0, 0)),
                  pl.BlockSpec((2, C), lambda b, n: (0, n))],
        out_specs=pl.BlockSpec((S, C), lambda b, n: (b, n)),
        out_shape=jax.ShapeDtypeStruct((T, R), BF16),
        scratch_shapes=[pltpu.VMEM((2, S, C), F32), pltpu.VMEM((2, S, C), F32)],
        compiler_params=_params(("parallel", "parallel"), vmem),
        name="rglru_core",
    )(xg, xg, conv_w.astype(F32), conv_b.reshape(1, R).astype(F32), w4, b4, coef)


def _first_top2(vals):
    m1 = vals[0]
    j1 = jnp.zeros(vals[0].shape, jnp.int32)
    for j in range(1, len(vals)):
        upd = vals[j] > m1
        m1 = jnp.where(upd, vals[j], m1)
        j1 = jnp.where(upd, j, j1)
    m2 = jnp.full(vals[0].shape, -jnp.inf, F32)
    j2 = jnp.zeros(vals[0].shape, jnp.int32)
    for j in range(len(vals)):
        upd = (j1 != j) & (vals[j] > m2)
        m2 = jnp.where(upd, vals[j], m2)
        j2 = jnp.where(upd, j, j2)
    return m1, j1, m2, j2


def _router_kernel(x_ref, rw_ref, rb_ref, idx_ref, gate_ref):
    G = N_GROUPS
    J = N_EXPERTS // G
    logits = lax.dot_general(rw_ref[...], x_ref[...], (((1,), (1,)), ((), ())), preferred_element_type=F32)
    s = jax.nn.sigmoid(logits)
    biased = s + rb_ref[...]
    a = [biased[j * G:(j + 1) * G, :] for j in range(J)]
    sv = [s[j * G:(j + 1) * G, :] for j in range(J)]
    m1, _, m2, _ = _first_top2(a)
    gscore = m1 + m2
    gidx = lax.broadcasted_iota(jnp.int32, gscore.shape, 0)
    gmax = jnp.max(gscore, axis=0, keepdims=True)
    gsel = jnp.min(jnp.where(gscore == gmax, gidx, G), axis=0, keepdims=True)
    mask = gidx == gsel
    av = [jnp.sum(jnp.where(mask, a[j], 0.0), axis=0, keepdims=True) for j in range(J)]
    sw = [jnp.sum(jnp.where(mask, sv[j], 0.0), axis=0, keepdims=True) for j in range(J)]
    _, j1, _, j2 = _first_top2(av)
    w1 = jnp.zeros_like(sw[0])
    w2 = jnp.zeros_like(sw[0])
    for j in range(J):
        w1 = jnp.where(j1 == j, sw[j], w1)
        w2 = jnp.where(j2 == j, sw[j], w2)
    tot = w1 + w2
    idx_ref[0:1, :] = gsel * J + j1
    idx_ref[1:2, :] = gsel * J + j2
    gate_ref[0:1, :] = w1 / tot
    gate_ref[1:2, :] = w2 / tot


def _router(xb, router_w, router_b):
    T, D = xb.shape
    G = N_GROUPS
    J = N_EXPERTS // G
    perm = jnp.asarray([(r % G) * J + r // G for r in range(N_EXPERTS)], jnp.int32)
    rw = router_w.T[perm].astype(BF16)
    rb = router_b[perm].reshape(N_EXPERTS, 1).astype(F32)
    tm = _tile(T, 1024, LANES)
    vmem = 2 * tm * D * 2 + 2 * N_EXPERTS * D * 2 + 64 * tm * 4 * 4 + 4 * 2**20
    return pl.pallas_call(
        _router_kernel,
        grid=(T // tm,),
        in_specs=[pl.BlockSpec((tm, D), lambda i: (i, 0)),
                  pl.BlockSpec((N_EXPERTS, D), lambda i: (0, 0)),
                  pl.BlockSpec((N_EXPERTS, 1), lambda i: (0, 0))],
        out_specs=[pl.BlockSpec((2, tm), lambda i: (0, i)), pl.BlockSpec((2, tm), lambda i: (0, i))],
        out_shape=[jax.ShapeDtypeStruct((2, T), jnp.int32), jax.ShapeDtypeStruct((2, T), F32)],
        compiler_params=_params(("parallel",), vmem),
        name="router",
    )(xb, rw, rb)


def _row_copy(src_hbm, src_row, dst, dst_row, sem):
    return pltpu.make_async_copy(src_hbm.at[pl.ds(src_row, 1)], dst.at[pl.ds(dst_row, 1)], sem)


def _gather_kernel(nused_ref, tok_ref, x_hbm, o_ref, buf, sem, *, tg):
    @pl.when(pl.program_id(0) < nused_ref[0])
    def _():
        def start(r, c):
            _row_copy(x_hbm, tok_ref[0, r], buf, r, sem).start()
            return c

        lax.fori_loop(0, tg, start, 0)

        def wait(r, c):
            _row_copy(x_hbm, 0, buf, r, sem).wait()
            return c

        lax.fori_loop(0, tg, wait, 0)
        o_ref[...] = buf[...].astype(o_ref.dtype)

    @pl.when(pl.program_id(0) >= nused_ref[0])
    def _():
        o_ref[...] = jnp.zeros(o_ref.shape, o_ref.dtype)


def _moe_gather(x, buf_tok, n_used_rows, rows):
    T, D = x.shape
    tg = MOE_TG
    nblk = rows // tg
    grid_spec = pltpu.PrefetchScalarGridSpec(
        num_scalar_prefetch=1,
        grid=(nblk,),
        in_specs=[pl.BlockSpec((None, 1, tg), lambda i, nu: (i, 0, 0), memory_space=pltpu.SMEM),
                  pl.BlockSpec(memory_space=pl.ANY)],
        out_specs=pl.BlockSpec((tg, D), lambda i, nu: (i, 0)),
        scratch_shapes=[pltpu.VMEM((tg, D), x.dtype), pltpu.SemaphoreType.DMA(())],
    )
    vmem = tg * D * 4 + 2 * tg * D * 2 + 2 * tg * D * 4 + 4 * 2**20
    return pl.pallas_call(
        functools.partial(_gather_kernel, tg=tg),
        grid_spec=grid_spec,
        out_shape=jax.ShapeDtypeStruct((rows, D), BF16),
        compiler_params=_params(("arbitrary",), vmem),
        name="moe_gather",
    )((n_used_rows // tg).reshape(1).astype(jnp.int32), buf_tok.reshape(nblk, 1, tg), x)


def _expert_kernel(blk_e_ref, nused_ref, x_ref, wg_ref, wu_ref, wd_ref, gw_ref, o_ref, hid_ref, *, nf, tf):
    del blk_e_ref
    b = pl.program_id(0)
    s = pl.program_id(1)
    used = b < nused_ref[0]

    @pl.when(used & (s < nf))
    def _():
        x = x_ref[...]
        g = jnp.dot(x, wg_ref[...].astype(BF16), preferred_element_type=F32)
        u = jnp.dot(x, wu_ref[...].astype(BF16), preferred_element_type=F32)
        hid_ref[s] = (jax.nn.silu(g) * u).astype(BF16)

    @pl.when(used & (s >= nf))
    def _():
        hidden = jnp.concatenate([hid_ref[f] for f in range(nf)], axis=1)
        acc = jnp.dot(hidden, wd_ref[...].astype(BF16), preferred_element_type=F32)
        o_ref[...] = acc * gw_ref[...]

    @pl.when(jnp.logical_not(used) & (s >= nf))
    def _():
        o_ref[...] = jnp.zeros(o_ref.shape, o_ref.dtype)


def _moe_experts(xs, w_gu, w_dn, buf_w, blk_e, n_used_blocks):
    rows, D = xs.shape
    E, _, F2 = w_gu.shape
    F = F2 // 2
    bm = MOE_BM
    tf = _tile(F, MOE_TF, LANES)
    nf = F // tf
    td = _tile(D, max(256, (4 * 2**20) // (F * 4)), 256 if D % 256 == 0 else LANES)
    nd = D // td
    nb = rows // bm

    def blk(b, nu):
        return jnp.minimum(b, nu[0] - 1)

    def ftile(b, s, nu):
        return jnp.where(b < nu[0], jnp.minimum(s, nf - 1), nf - 1)

    def dtile(b, s, nu):
        return jnp.where(b < nu[0], jnp.maximum(s - nf, 0), nd - 1)

    grid_spec = pltpu.PrefetchScalarGridSpec(
        num_scalar_prefetch=2,
        grid=(nb, nf + nd),
        in_specs=[
            pl.BlockSpec((bm, D), lambda b, s, be, nu: (blk(b, nu), 0)),
            pl.BlockSpec((None, D, tf), lambda b, s, be, nu: (be[blk(b, nu)], 0, ftile(b, s, nu))),
            pl.BlockSpec((None, D, tf), lambda b, s, be, nu: (be[blk(b, nu)], 0, nf + ftile(b, s, nu))),
            pl.BlockSpec((None, F, td), lambda b, s, be, nu: (be[blk(b, nu)], 0, dtile(b, s, nu))),
            pl.BlockSpec((bm, 1), lambda b, s, be, nu: (blk(b, nu), 0)),
        ],
        out_specs=pl.BlockSpec((bm, td), lambda b, s, be, nu: (b, jnp.maximum(s - nf, 0))),
        scratch_shapes=[pltpu.VMEM((nf, bm, tf), BF16)],
    )
    vmem = (2 * bm * D * 2 + 2 * 2 * D * tf * 4 + 2 * D * tf * 2 + 2 * F * td * 4 + F * td * 2
            + bm * F * 2 + 4 * bm * td * 4 + 4 * bm * tf * 4 + 4 * 2**20)
    return pl.pallas_call(
        functools.partial(_expert_kernel, nf=nf, tf=tf),
        grid_spec=grid_spec,
        out_shape=jax.ShapeDtypeStruct((rows, D), F32),
        compiler_params=_params(("arbitrary", "arbitrary"), vmem),
        name="moe_experts",
    )(blk_e, n_used_blocks.reshape(1).astype(jnp.int32), xs, w_gu, w_gu, w_dn, buf_w.reshape(rows, 1))


def _combine_ln_kernel(pos_ref, x_ref, ys_hbm, g_ref, b_ref, o_ref, ob_ref, buf, sem, *, tc):
    def start(r, c):
        _row_copy(ys_hbm, pos_ref[0, r], buf.at[0], r, sem).start()
        _row_copy(ys_hbm, pos_ref[0, tc + r], buf.at[1], r, sem).start()
        return c

    lax.fori_loop(0, tc, start, 0)

    def wait(r, c):
        _row_copy(ys_hbm, 0, buf.at[0], r, sem).wait()
        _row_copy(ys_hbm, 0, buf.at[1], r, sem).wait()
        return c

    lax.fori_loop(0, tc, wait, 0)
    y = DEEPNORM_ALPHA * x_ref[...] + (buf[0] + buf[1])
    out = _layer_norm_rows(y, g_ref[...], b_ref[...])
    o_ref[...] = out
    ob_ref[...] = out.astype(BF16)


def _moe_combine_ln(x, ys, dest, g, b):
    T, D = x.shape
    tc = _tile(T, 256, SUBLANES)
    nblk = T // tc
    pos = dest.reshape(2, nblk, tc).transpose(1, 0, 2).reshape(nblk, 1, 2 * tc)
    row = pl.BlockSpec((tc, D), lambda i: (i, 0))
    vec = pl.BlockSpec((1, D), lambda i: (0, 0))
    vmem = 2 * tc * D * 4 + 2 * tc * D * 4 + 2 * tc * D * (4 + 2) + 4 * tc * D * 4 + 4 * 2**20
    return pl.pallas_call(
        functools.partial(_combine_ln_kernel, tc=tc),
        grid=(nblk,),
        in_specs=[pl.BlockSpec((None, 1, 2 * tc), lambda i: (i, 0, 0), memory_space=pltpu.SMEM),
                  row, pl.BlockSpec(memory_space=pl.ANY), vec, vec],
        out_specs=[row, row],
        out_shape=[jax.ShapeDtypeStruct((T, D), F32), jax.ShapeDtypeStruct((T, D), BF16)],
        scratch_shapes=[pltpu.VMEM((2, tc, D), F32), pltpu.SemaphoreType.DMA(())],
        compiler_params=_params(("arbitrary",), vmem),
        name="moe_combine_ln",
    )(pos, x, ys, g.reshape(1, D), b.reshape(1, D))


def _moe_layout(idx, gate, T):
    E = N_EXPERTS
    bm = MOE_BM
    nb = -(-2 * T // bm) + E
    rows = nb * bm
    flat_e = idx.reshape(-1)
    flat_w = gate.reshape(-1)
    tok = jnp.tile(jnp.arange(T, dtype=jnp.int32), 2)
    onehot = (flat_e[:, None] == jnp.arange(E, dtype=jnp.int32)[None, :]).astype(jnp.int32)
    csum = jnp.cumsum(onehot, axis=0)
    rank = jnp.sum(onehot * csum, axis=1) - 1
    counts = csum[-1]
    padded = (counts + bm - 1) // bm * bm
    pad_end = jnp.cumsum(padded)
    pad_start = pad_end - padded
    dest = (pad_start[flat_e] + rank).astype(jnp.int32)
    buf_tok = jnp.zeros((rows,), jnp.int32).at[dest].set(tok)
    buf_w = jnp.zeros((rows,), F32).at[dest].set(flat_w)
    blk_e = jnp.minimum(jnp.searchsorted(pad_end, jnp.arange(nb, dtype=jnp.int32) * bm, side="right"),
                        E - 1).astype(jnp.int32)
    n_used_rows = pad_end[-1].astype(jnp.int32)
    return dest, buf_tok, buf_w, blk_e, n_used_rows, rows


def _routed_experts_ln(x, xb, router_w, router_b, w_gu, w_dn, g, b):
    T, D = x.shape
    idx, gate = _router(xb, router_w, router_b)
    dest, buf_tok, buf_w, blk_e, n_used_rows, rows = _moe_layout(idx, gate, T)
    xs = _moe_gather(x, buf_tok, n_used_rows, rows)
    ys = _moe_experts(xs, w_gu, w_dn, buf_w, blk_e, n_used_rows // MOE_BM)
    return _moe_combine_ln(x, ys, dest, g, b)


def kernel(x_prompt, x_sample, mem_prompt, mem_sample, da_w_qkv, da_lam, da_subln_g, da_w_o, sg_w_in, sg_b_in, sg_ln_g, sg_ln_b, sg_w_s, sg_b_s, sg_w_o, rg_w_in, rg_conv_w, rg_conv_b, rg_w_a, rg_b_a, rg_w_i, rg_b_i, rg_lambda, rg_w_o, ca_w_q, ca_w_kv, ca_w_o, router_w, router_b, moe_w_gu, moe_w_dn, ln_g, ln_b):
    Bp, S, D = x_prompt.shape
    Bs = x_sample.shape[0]
    assert x_sample.shape[1] == S, "both request groups must share the sequence length"
    B = Bp + Bs
    T = B * S
    M = mem_prompt.shape[1]
    x = jnp.concatenate([x_prompt.reshape(Bp * S, D), x_sample.reshape(Bs * S, D)], axis=0)
    xb = x.astype(BF16)
    mem = jnp.concatenate([mem_prompt.reshape(Bp * M, D), mem_sample.reshape(Bs * M, D)], axis=0).astype(BF16)
    slopes = jnp.asarray([2.0 ** (-8.0 * (h + 1) / DA_HEADS) for h in range(DA_HEADS)], F32)

    for i in range(DEPTH):
        kind, slot = i % N_MIXERS, i // N_MIXERS
        if kind == 0:
            lam_init = 0.8 - 0.6 * math.exp(-0.3 * i)
            lf = da_lam[slot].astype(F32)
            lam_full = (jnp.exp(jnp.sum(lf[0] * lf[1])) - jnp.exp(jnp.sum(lf[2] * lf[3])) + lam_init).reshape(1)
            qkv = _mm(xb, da_w_qkv[slot], out_dtype=BF16, name="da_qkv")
            o = _diff_attention(qkv, B, S, slopes, lam_full, da_subln_g[slot], lam_init)
            h = _mm(o, da_w_o[slot], name="da_out")
        elif kind == 1:
            z = _mm(xb, sg_w_in[slot], bias=sg_b_in[slot], act="gelu", out_dtype=BF16, name="sg_in")
            gated = _spatial_gate(z, sg_ln_g[slot], sg_ln_b[slot], sg_w_s[slot], sg_b_s[slot])
            h = _mm(gated, sg_w_o[slot], name="sg_out")
        else:
            xg = _mm(xb, rg_w_in[slot], name="rg_in")
            hr = _rglru_core(xg, B, S, rg_conv_w[slot], rg_conv_b[slot], rg_w_a[slot], rg_b_a[slot],
                             rg_w_i[slot], rg_b_i[slot], rg_lambda[slot])
            h = _mm(hr, rg_w_o[slot], name="rg_out")
        x, xb = _add_ln(x, h, ln_g[i, 0], ln_b[i, 0])

        q = _mm(xb, ca_w_q[i], out_dtype=BF16, name="ca_q")
        kv = _mm(mem, ca_w_kv[i], out_dtype=BF16, name="ca_kv")
        o = _memory_attention_core(q, kv, B, S, M)
        h = _mm(o, ca_w_o[i], name="ca_out")
        x, xb = _add_ln(x, h, ln_g[i, 1], ln_b[i, 1])

        x, xb = _routed_experts_ln(x, xb, router_w, router_b, moe_w_gu[i], moe_w_dn[i], ln_g[i, 2], ln_b[i, 2])

    y = x.reshape(B, S, D)
    return (y[:Bp], y[Bp:])
```

```python
import functools
import math

import jax
import jax.numpy as jnp
from jax import lax
from jax.experimental import pallas as pl
from jax.experimental.pallas import tpu as pltpu

DEPTH = 4
N_MIXERS = 3
DA_HEADS = 16
SG_CHUNK = 128
SG_GROUPS = 16
RG_BLOCKS = 16
RG_C = 8.0
CA_HEADS = 4
N_EXPERTS = 32
N_GROUPS = 8
LN_EPS = 1e-5
DEEPNORM_ALPHA = (2 * DEPTH) ** 0.25
LOG2E = math.log2(math.e)

V7X_VMEM_BYTES = 64 * 1024 * 1024
VMEM_CAP = V7X_VMEM_BYTES - 6 * 1024 * 1024
SUBLANES = 8
LANES = 128

MOE_BM = 512
MOE_TF = 256
MOE_TG = 256

BF16 = jnp.bfloat16
F32 = jnp.float32


def _tile(dim, pref, align):
    if dim <= pref:
        return dim
    t = (pref // align) * align
    while t > align and dim % t:
        t -= align
    assert dim % t == 0, (dim, pref, align)
    return t


def _params(sem, vmem_bytes):
    return pltpu.CompilerParams(dimension_semantics=sem,
                                vmem_limit_bytes=int(min(max(vmem_bytes, 16 * 2**20), VMEM_CAP)))


def _mm_kernel(*refs, act, has_bias, has_scale):
    x_ref, w_ref = refs[0], refs[1]
    o_ref, wbf_ref = refs[-2], refs[-1]
    extra = list(refs[2:-2])

    @pl.when(pl.program_id(1) == 0)
    def _():
        wbf_ref[...] = w_ref[...].astype(BF16)

    acc = jnp.dot(x_ref[...].astype(BF16), wbf_ref[...], preferred_element_type=F32)
    if has_bias:
        acc = acc + extra.pop(0)[...]
    if has_scale:
        acc = acc * extra.pop(0)[...]
    if act == "gelu":
        acc = jax.nn.gelu(acc)
    o_ref[...] = acc.astype(o_ref.dtype)


def _mm(x, w, layer, bias=None, col_scale=None, act=None, out_dtype=F32, name="mm"):
    M, K = x.shape
    _, K2, N = w.shape
    assert K == K2
    xb = x.dtype.itemsize
    ob = jnp.dtype(out_dtype).itemsize
    tm = _tile(M, max(256, (8 * 2**20) // (K * xb)), 256 if M % 256 == 0 else SUBLANES)
    tn = _tile(N, max(256, (8 * 2**20) // (K * 4)), 256 if N % 256 == 0 else LANES)
    grid = (N // tn, M // tm)
    in_specs = [pl.BlockSpec((tm, K), lambda j, i: (i, 0)),
                pl.BlockSpec((None, K, tn), lambda j, i: (layer, 0, j))]
    args = [x, w]
    for vec in (bias, col_scale):
        if vec is not None:
            in_specs.append(pl.BlockSpec((1, tn), lambda j, i: (0, j)))
            args.append(vec.reshape(1, N).astype(F32))
    vmem = 2 * tm * K * xb + 2 * K * tn * 4 + K * tn * 2 + 2 * tm * tn * ob + 2 * tm * tn * 4 + 4 * 2**20
    return pl.pallas_call(
        functools.partial(_mm_kernel, act=act, has_bias=bias is not None, has_scale=col_scale is not None),
        grid=grid,
        in_specs=in_specs,
        out_specs=pl.BlockSpec((tm, tn), lambda j, i: (i, j)),
        out_shape=jax.ShapeDtypeStruct((M, N), out_dtype),
        scratch_shapes=[pltpu.VMEM((K, tn), BF16)],
        compiler_params=_params(("parallel", "arbitrary"), vmem),
        name=name,
    )(*args)


def _layer_norm_rows(y, g, b):
    mu = jnp.mean(y, axis=-1, keepdims=True)
    yc = y - mu
    var = jnp.mean(yc * yc, axis=-1, keepdims=True)
    return yc * lax.rsqrt(var + LN_EPS) * g + b


def _pack_slabs(out, xp_ref):
    tm, D = out.shape
    half = D // 2
    ns = half // LANES
    lo = pltpu.bitcast(out[:, :half].astype(BF16).astype(F32), jnp.uint32) >> 16
    hi = pltpu.bitcast(out[:, half:].astype(BF16).astype(F32), jnp.uint32) & jnp.uint32(0xFFFF0000)
    w = hi | lo
    for c in range(ns):
        xp_ref[pl.ds(c, tm, stride=ns), :] = w[:, c * LANES:(c + 1) * LANES]


def _add_ln_kernel(x_ref, h_ref, g_ref, b_ref, o_ref, ob_ref, *maybe_xp_ref):
    y = DEEPNORM_ALPHA * x_ref[...] + h_ref[...].astype(F32)
    out = _layer_norm_rows(y, g_ref[...], b_ref[...])
    o_ref[...] = out
    ob_ref[...] = out.astype(BF16)
    if maybe_xp_ref:
        _pack_slabs(out, maybe_xp_ref[0])


def _add_ln(x, h, g, b, emit_packed=False):
    T, D = x.shape
    tm = _tile(T, 256, SUBLANES)
    ns = D // 2 // LANES
    row = pl.BlockSpec((tm, D), lambda i: (i, 0))
    vec = pl.BlockSpec((1, D), lambda i: (0, 0))
    out_specs = [row, row]
    out_shape = [jax.ShapeDtypeStruct((T, D), F32), jax.ShapeDtypeStruct((T, D), BF16)]
    if emit_packed:
        out_specs.append(pl.BlockSpec((tm * ns, LANES), lambda i: (i, 0)))
        out_shape.append(jax.ShapeDtypeStruct((T * ns, LANES), jnp.uint32))
    vmem = 2 * tm * D * (4 + h.dtype.itemsize + 4 + 2 + 2) + 6 * tm * D * 4
    return pl.pallas_call(
        _add_ln_kernel,
        grid=(T // tm,),
        in_specs=[row, row, vec, vec],
        out_specs=out_specs,
        out_shape=out_shape,
        compiler_params=_params(("parallel",), vmem),
        name="add_ln",
    )(x, h, g.reshape(1, D), b.reshape(1, D))


def _da_kernel(slope_ref, lam_ref, q_ref, k_ref, v_ref, g_ref, o_ref, bias_ref, *, tq, sub, lam_init):
    h = pl.program_id(0)
    qi = pl.program_id(1)
    S = k_ref.shape[0]
    dh = q_ref.shape[1] // 2

    @pl.when(pl.program_id(2) == 0)
    def _():
        qpos = qi * tq + lax.broadcasted_iota(jnp.int32, (tq, S), 0)
        kpos = lax.broadcasted_iota(jnp.int32, (tq, S), 1)
        bias_ref[...] = jnp.abs(qpos - kpos).astype(F32) * (-LOG2E * slope_ref[h])

    lam = lam_ref[0]
    k = k_ref[...]
    v = v_ref[...]

    def unnormalised(q_rows, rows, c):
        s = lax.dot_general(q_rows[:, c * dh:(c + 1) * dh], k[:, c * dh:(c + 1) * dh],
                            (((1,), (1,)), ((), ())), preferred_element_type=F32)
        xs = [s[:, j * LANES:(j + 1) * LANES] + bias_ref[rows, j * LANES:(j + 1) * LANES]
              for j in range(S // LANES)]
        m = functools.reduce(jnp.maximum, xs)
        m = jnp.max(m, axis=-1, keepdims=True)
        es = [jnp.exp2(x - m) for x in xs]
        l = jnp.sum(functools.reduce(lambda p, t: p + t, es), axis=-1, keepdims=True)
        e = jnp.concatenate([t.astype(BF16) for t in es], axis=1)
        return jnp.dot(e, v, preferred_element_type=F32), l

    for r0 in range(0, tq, sub):
        rows = slice(r0, r0 + sub)
        q_rows = q_ref[rows, :]
        pv1, l1 = unnormalised(q_rows, rows, 0)
        pv2, l2 = unnormalised(q_rows, rows, 1)
        o = pv1 * (1.0 / l1) - pv2 * (lam / l2)
        ms = jnp.mean(o * o, axis=-1, keepdims=True)
        o = o * lax.rsqrt(ms + LN_EPS) * g_ref[...] * (1.0 - lam_init)
        o_ref[rows, :] = o.astype(o_ref.dtype)


def _diff_attention(qkv, B, S, slopes, lam_full, subln_g, lam_init):
    T, D3 = qkv.shape
    D = D3 // 3
    H = DA_HEADS
    hw = D // H
    tq = _tile(S, 512, SUBLANES)
    nq = S // tq
    smem = pl.BlockSpec(memory_space=pltpu.SMEM)
    vmem = 10 * tq * S * 4 + 8 * S * hw * 2 + 4 * 2**20
    return pl.pallas_call(
        functools.partial(_da_kernel, tq=tq, sub=_tile(tq, 128, SUBLANES), lam_init=lam_init),
        grid=(H, nq, B),
        in_specs=[smem, smem,
                  pl.BlockSpec((tq, hw), lambda h, qi, b: (b * nq + qi, h)),
                  pl.BlockSpec((S, hw), lambda h, qi, b: (b, H + h)),
                  pl.BlockSpec((S, hw), lambda h, qi, b: (b, 2 * H + h)),
                  pl.BlockSpec((1, hw), lambda h, qi, b: (0, 0))],
        out_specs=pl.BlockSpec((tq, hw), lambda h, qi, b: (b * nq + qi, h)),
        out_shape=jax.ShapeDtypeStruct((T, D), BF16),
        scratch_shapes=[pltpu.VMEM((tq, S), F32)],
        compiler_params=_params(("parallel", "parallel", "arbitrary"), vmem),
        name="diff_attn",
    )(slopes, lam_full, qkv, qkv, qkv, subln_g.reshape(1, hw).astype(F32))


def _ca_kernel(q_ref, k_ref, v_ref, o_ref):
    dh = q_ref.shape[1]
    s = lax.dot_general(q_ref[...], k_ref[...], (((1,), (1,)), ((), ())),
                        preferred_element_type=F32) * (dh ** -0.5)
    m = jnp.max(s, axis=-1, keepdims=True)
    e = jnp.exp(s - m)
    p = e * (1.0 / jnp.sum(e, axis=-1, keepdims=True))
    o_ref[...] = jnp.dot(p.astype(BF16), v_ref[...], preferred_element_type=F32).astype(o_ref.dtype)


def _memory_attention_core(q, kv, B, S, M):
    T, D = q.shape
    H = CA_HEADS
    dh = D // H
    tq = _tile(S, 512, SUBLANES)
    nq = S // tq
    vmem = 4 * tq * dh * 2 * 2 + 4 * M * dh * 2 + 4 * tq * M * 4 + 2 * tq * dh * 4 + 4 * 2**20
    return pl.pallas_call(
        _ca_kernel,
        grid=(B, H, nq),
        in_specs=[pl.BlockSpec((tq, dh), lambda b, h, qi: (b * nq + qi, h)),
                  pl.BlockSpec((M, dh), lambda b, h, qi: (b, h)),
                  pl.BlockSpec((M, dh), lambda b, h, qi: (b, H + h))],
        out_specs=pl.BlockSpec((tq, dh), lambda b, h, qi: (b * nq + qi, h)),
        out_shape=jax.ShapeDtypeStruct((T, D), BF16),
        compiler_params=_params(("parallel", "parallel", "parallel"), vmem),
        name="mem_attn",
    )(q, kv, kv)


def _sg_kernel(u_ref, v_ref, g_ref, b_ref, ws_ref, bs_ref, o_ref, *, n_chunks):
    W = v_ref.shape[1]
    G = ws_ref.shape[0]
    gw = W // G
    vn = _layer_norm_rows(v_ref[...].astype(F32), g_ref[...], b_ref[...]).astype(BF16)
    for c in range(n_chunks):
        rows = slice(c * SG_CHUNK, (c + 1) * SG_CHUNK)
        for g in range(G):
            cols = slice(g * gw, (g + 1) * gw)
            y = jnp.dot(ws_ref[g], vn[rows, cols], preferred_element_type=F32) + bs_ref[:, g:g + 1]
            o_ref[rows, cols] = (u_ref[rows, cols].astype(F32) * y).astype(o_ref.dtype)


def _spatial_gate(z, ln_g, ln_b, w_s, b_s):
    T, W2 = z.shape
    W = W2 // 2
    n_chunks = 2 if T % (2 * SG_CHUNK) == 0 else 1
    R = n_chunks * SG_CHUNK
    G = w_s.shape[0]
    zb = z.dtype.itemsize
    vmem = 4 * R * W * zb + 2 * R * W * 2 + 4 * R * W * 4 + 4 * 2**20
    return pl.pallas_call(
        functools.partial(_sg_kernel, n_chunks=n_chunks),
        grid=(T // R,),
        in_specs=[pl.BlockSpec((R, W), lambda i: (i, 0)),
                  pl.BlockSpec((R, W), lambda i: (i, 1)),
                  pl.BlockSpec((1, W), lambda i: (0, 0)),
                  pl.BlockSpec((1, W), lambda i: (0, 0)),
                  pl.BlockSpec((G, SG_CHUNK, SG_CHUNK), lambda i: (0, 0, 0)),
                  pl.BlockSpec((SG_CHUNK, G), lambda i: (0, 0))],
        out_specs=pl.BlockSpec((R, W), lambda i: (i, 0)),
        out_shape=jax.ShapeDtypeStruct((T, W), BF16),
        compiler_params=_params(("parallel",), vmem),
        name="spatial_gate",
    )(z, z, ln_g.reshape(1, W), ln_b.reshape(1, W), w_s.astype(BF16), b_s.T.astype(F32))


def _rg_kernel(xr_ref, g_ref, cw_ref, cb_ref, w4_ref, b4_ref, c_ref, o_ref, a_s, b_s):
    S, C = xr_ref.shape
    xr = xr_ref[...].astype(F32)
    t = lax.broadcasted_iota(jnp.int32, (S, C), 0)
    x_m2 = jnp.where(t >= 2, pltpu.roll(xr, 2, 0), 0.0)
    x_m1 = jnp.where(t >= 1, pltpu.roll(xr, 1, 0), 0.0)
    x_p1 = jnp.where(t < S - 1, pltpu.roll(xr, S - 1, 0), 0.0)
    xc = x_m2 * cw_ref[0:1, :] + x_m1 * cw_ref[1:2, :] + xr * cw_ref[2:3, :] + x_p1 * cw_ref[3:4, :] + cb_ref[...]
    pre = jnp.dot(xc.astype(BF16), w4_ref[...], preferred_element_type=F32) + b4_ref[...]
    for d in range(2):
        r = jax.nn.sigmoid(pre[:, (2 * d) * C:(2 * d + 1) * C])
        i = jax.nn.sigmoid(pre[:, (2 * d + 1) * C:(2 * d + 2) * C])
        a = jnp.exp(c_ref[d:d + 1, :] * r)
        a_s[d] = a
        b_s[d] = jnp.sqrt(jnp.maximum(1.0 - a * a, 0.0)) * (i * xc)

    row = lax.broadcasted_iota(jnp.int32, (SUBLANES, C), 0)
    n_groups = S // SUBLANES

    def group_scan(A, Bv, reverse):
        for sh in (1, 2, 4):
            if reverse:
                ok = row < SUBLANES - sh
                rs = SUBLANES - sh
            else:
                ok = row >= sh
                rs = sh
            A_sh = jnp.where(ok, pltpu.roll(A, rs, 0), 1.0)
            B_sh = jnp.where(ok, pltpu.roll(Bv, rs, 0), 0.0)
            Bv = A * B_sh + Bv
            A = A * A_sh
        return A, Bv

    def body(gi, carry):
        hf, hb = carry
        r0 = pl.multiple_of(gi * SUBLANES, SUBLANES)
        A, Bv = group_scan(a_s[0, pl.ds(r0, SUBLANES), :], b_s[0, pl.ds(r0, SUBLANES), :], False)
        Hf = A * hf + Bv
        b_s[0, pl.ds(r0, SUBLANES), :] = Hf
        r1 = pl.multiple_of((n_groups - 1 - gi) * SUBLANES, SUBLANES)
        A, Bv = group_scan(a_s[1, pl.ds(r1, SUBLANES), :], b_s[1, pl.ds(r1, SUBLANES), :], True)
        Hb = A * hb + Bv
        b_s[1, pl.ds(r1, SUBLANES), :] = Hb
        return Hf[SUBLANES - 1:SUBLANES, :], Hb[0:1, :]

    zero = jnp.zeros((1, C), F32)
    lax.fori_loop(0, n_groups, body, (zero, zero), unroll=2)
    gate = jax.nn.gelu(g_ref[...].astype(F32))
    o_ref[...] = ((b_s[0] + b_s[1]) * gate).astype(o_ref.dtype)


def _rglru_core(xg, B, S, conv_w, conv_b, w_a, b_a, w_i, b_i, lam):
    T, R2 = xg.shape
    R = R2 // 2
    nb = RG_BLOCKS
    C = R // nb
    w4 = jnp.concatenate([w_a[0], w_i[0], w_a[1], w_i[1]], axis=-1).astype(BF16)
    b4 = jnp.stack([b_a[0].reshape(nb, C), b_i[0].reshape(nb, C), b_a[1].reshape(nb, C), b_i[1].reshape(nb, C)],
                   axis=1).reshape(nb, 1, 4 * C).astype(F32)
    coef = (-RG_C) * jax.nn.softplus(-lam.astype(F32))
    xb = xg.dtype.itemsize
    vmem = 4 * S * C * xb + 2 * S * C * 2 + 4 * S * C * 4 + 3 * S * 4 * C * 4 + 6 * S * C * 4 + 4 * 2**20
    return pl.pallas_call(
        _rg_kernel,
        grid=(B, nb),
        in_specs=[pl.BlockSpec((S, C), lambda b, n: (b, n)),
                  pl.BlockSpec((S, C), lambda b, n: (b, nb + n)),
                  pl.BlockSpec((4, C), lambda b, n: (0, n)),
                  pl.BlockSpec((1, C), lambda b, n: (0, n)),
                  pl.BlockSpec((None, C, 4 * C), lambda b, n: (n, 0, 0)),
                  pl.BlockSpec((None, 1, 4 * C), lambda b, n: (n, 0, 0)),
                  pl.BlockSpec((2, C), lambda b, n: (0, n))],
        out_specs=pl.BlockSpec((S, C), lambda b, n: (b, n)),
        out_shape=jax.ShapeDtypeStruct((T, R), BF16),
        scratch_shapes=[pltpu.VMEM((2, S, C), F32), pltpu.VMEM((2, S, C), F32)],
        compiler_params=_params(("parallel", "parallel"), vmem),
        name="rglru_core",
    )(xg, xg, conv_w.astype(F32), conv_b.reshape(1, R).astype(F32), w4, b4, coef)


def _first_top2(vals):
    m1 = vals[0]
    j1 = jnp.zeros(vals[0].shape, jnp.int32)
    for j in range(1, len(vals)):
        upd = vals[j] > m1
        m1 = jnp.where(upd, vals[j], m1)
        j1 = jnp.where(upd, j, j1)
    m2 = jnp.full(vals[0].shape, -jnp.inf, F32)
    j2 = jnp.zeros(vals[0].shape, jnp.int32)
    for j in range(len(vals)):
        upd = (j1 != j) & (vals[j] > m2)
        m2 = jnp.where(upd, vals[j], m2)
        j2 = jnp.where(upd, j, j2)
    return m1, j1, m2, j2


def _router_kernel(x_ref, rw_ref, rb_ref, idx_ref, gate_ref):
    G = N_GROUPS
    J = N_EXPERTS // G
    logits = lax.dot_general(rw_ref[...], x_ref[...], (((1,), (1,)), ((), ())), preferred_element_type=F32)
    s = jax.nn.sigmoid(logits)
    biased = s + rb_ref[...]
    a = [biased[j * G:(j + 1) * G, :] for j in range(J)]
    sv = [s[j * G:(j + 1) * G, :] for j in range(J)]
    m1, _, m2, _ = _first_top2(a)
    gscore = m1 + m2
    gidx = lax.broadcasted_iota(jnp.int32, gscore.shape, 0)
    gmax = jnp.max(gscore, axis=0, keepdims=True)
    gsel = jnp.min(jnp.where(gscore == gmax, gidx, G), axis=0, keepdims=True)
    mask = gidx == gsel
    av = [jnp.sum(jnp.where(mask, a[j], 0.0), axis=0, keepdims=True) for j in range(J)]
    sw = [jnp.sum(jnp.where(mask, sv[j], 0.0), axis=0, keepdims=True) for j in range(J)]
    _, j1, _, j2 = _first_top2(av)
    w1 = jnp.zeros_like(sw[0])
    w2 = jnp.zeros_like(sw[0])
    for j in range(J):
        w1 = jnp.where(j1 == j, sw[j], w1)
        w2 = jnp.where(j2 == j, sw[j], w2)
    tot = w1 + w2
    idx_ref[0:1, :] = gsel * J + j1
    idx_ref[1:2, :] = gsel * J + j2
    gate_ref[0:1, :] = w1 / tot
    gate_ref[1:2, :] = w2 / tot


def _router(xb, router_w, router_b):
    T, D = xb.shape
    G = N_GROUPS
    J = N_EXPERTS // G
    perm = jnp.asarray([(r % G) * J + r // G for r in range(N_EXPERTS)], jnp.int32)
    rw = router_w.T[perm].astype(BF16)
    rb = router_b[perm].reshape(N_EXPERTS, 1).astype(F32)
    tm = _tile(T, 1024, LANES)
    vmem = 2 * tm * D * 2 + 2 * N_EXPERTS * D * 2 + 64 * tm * 4 * 4 + 4 * 2**20
    return pl.pallas_call(
        _router_kernel,
        grid=(T // tm,),
        in_specs=[pl.BlockSpec((tm, D), lambda i: (i, 0)),
                  pl.BlockSpec((N_EXPERTS, D), lambda i: (0, 0)),
                  pl.BlockSpec((N_EXPERTS, 1), lambda i: (0, 0))],
        out_specs=[pl.BlockSpec((2, tm), lambda i: (0, i)), pl.BlockSpec((2, tm), lambda i: (0, i))],
        out_shape=[jax.ShapeDtypeStruct((2, T), jnp.int32), jax.ShapeDtypeStruct((2, T), F32)],
        compiler_params=_params(("parallel",), vmem),
        name="router",
    )(xb, rw, rb)


def _slab_copy(xp_hbm, tok, buf, slot, r, sem, ns, pitch):
    return pltpu.make_async_copy(xp_hbm.at[pl.ds(tok * ns, ns)], buf.at[slot, pl.ds(r * pitch, ns)], sem.at[slot])


def _gather_kernel(nused_ref, tok_ref, tok_next_ref, xp_hbm, o_ref, buf, sem, *, tg, ns, pitch):
    i = pl.program_id(0)
    nu = nused_ref[0]
    half = ns * LANES

    def issue(tref, slot):
        def body(r, c):
            _slab_copy(xp_hbm, tref[0, r], buf, slot, r, sem, ns, pitch).start()
            return c

        lax.fori_loop(0, tg, body, 0, unroll=8)

    def consume(slot):
        def body(r, c):
            _slab_copy(xp_hbm, 0, buf, slot, r, sem, ns, pitch).wait()
            return c

        lax.fori_loop(0, tg, body, 0, unroll=8)
        for c in range(ns):
            w = buf[slot, pl.ds(c, tg, stride=pitch), :]
            lo = pltpu.bitcast(w << 16, F32)
            hi = pltpu.bitcast(w & jnp.uint32(0xFFFF0000), F32)
            o_ref[:, c * LANES:(c + 1) * LANES] = lo.astype(o_ref.dtype)
            o_ref[:, half + c * LANES:half + (c + 1) * LANES] = hi.astype(o_ref.dtype)

    for slot in range(2):
        @pl.when((i < nu) & (i % 2 == slot))
        def _(slot=slot):
            if slot == 0:
                @pl.when(i == 0)
                def _():
                    issue(tok_ref, 0)

            @pl.when(i + 1 < nu)
            def _():
                issue(tok_next_ref, 1 - slot)

            consume(slot)

    @pl.when(i >= nu)
    def _():
        o_ref[...] = jnp.zeros(o_ref.shape, o_ref.dtype)


def _moe_gather(xp, buf_tok, n_used_rows, rows, D):
    ns = D // 2 // LANES
    pitch = ns + SUBLANES
    tg = MOE_TG
    nblk = rows // tg
    tok3 = buf_tok.reshape(nblk, 1, tg)
    grid_spec = pltpu.PrefetchScalarGridSpec(
        num_scalar_prefetch=1,
        grid=(nblk,),
        in_specs=[pl.BlockSpec((None, 1, tg), lambda i, nu: (i, 0, 0), memory_space=pltpu.SMEM),
                  pl.BlockSpec((None, 1, tg), lambda i, nu: (jnp.minimum(i + 1, nblk - 1), 0, 0),
                               memory_space=pltpu.SMEM),
                  pl.BlockSpec(memory_space=pl.ANY)],
        out_specs=pl.BlockSpec((tg, D), lambda i, nu: (i, 0)),
        scratch_shapes=[pltpu.VMEM((2, tg * pitch, LANES), jnp.uint32), pltpu.SemaphoreType.DMA((2,))],
    )
    vmem = 2 * tg * pitch * LANES * 4 + 2 * tg * D * 2 + 4 * tg * D * 4 + 4 * 2**20
    return pl.pallas_call(
        functools.partial(_gather_kernel, tg=tg, ns=ns, pitch=pitch),
        grid_spec=grid_spec,
        out_shape=jax.ShapeDtypeStruct((rows, D), BF16),
        compiler_params=_params(("arbitrary",), vmem),
        name="moe_gather",
    )((n_used_rows // tg).reshape(1).astype(jnp.int32), tok3, tok3, xp)


def _expert_changed(be_ref, b):
    return (b == 0) | (be_ref[b] != be_ref[jnp.maximum(b - 1, 0)])


def _expert_up_kernel(be_ref, nu_ref, x_ref, wg_ref, wu_ref, h_ref, wgb, wub):
    b = pl.program_id(1)
    used = b < nu_ref[0]

    @pl.when(used & _expert_changed(be_ref, b))
    def _():
        wgb[...] = wg_ref[...].astype(BF16)
        wub[...] = wu_ref[...].astype(BF16)

    @pl.when(used)
    def _():
        x = x_ref[...]
        g = jnp.dot(x, wgb[...], preferred_element_type=F32)
        u = jnp.dot(x, wub[...], preferred_element_type=F32)
        h_ref[...] = (jax.nn.silu(g) * u).astype(h_ref.dtype)

    @pl.when(jnp.logical_not(used))
    def _():
        h_ref[...] = jnp.zeros(h_ref.shape, h_ref.dtype)


def _expert_down_kernel(be_ref, nu_ref, h_ref, wd_ref, o_ref, wdb):
    b = pl.program_id(1)
    used = b < nu_ref[0]

    @pl.when(used & _expert_changed(be_ref, b))
    def _():
        wdb[...] = wd_ref[...].astype(BF16)

    @pl.when(used)
    def _():
        o_ref[...] = jnp.dot(h_ref[...], wdb[...], preferred_element_type=F32).astype(o_ref.dtype)

    @pl.when(jnp.logical_not(used))
    def _():
        o_ref[...] = jnp.zeros(o_ref.shape, o_ref.dtype)


def _moe_experts(xs, w_gu, w_dn, layer, blk_e, n_used_blocks):
    rows, D = xs.shape
    F = w_gu.shape[-1] // 2
    bm = MOE_BM
    nb = rows // bm
    tf = _tile(F, MOE_TF, LANES)
    nf = F // tf
    td = _tile(D, max(256, (8 * 2**20) // (F * 4)), 256 if D % 256 == 0 else LANES)
    nd = D // td
    nu = n_used_blocks.reshape(1).astype(jnp.int32)

    def blk(b, nu_ref):
        return jnp.minimum(b, nu_ref[0] - 1)

    up_spec = pltpu.PrefetchScalarGridSpec(
        num_scalar_prefetch=2,
        grid=(nf, nb),
        in_specs=[
            pl.BlockSpec((bm, D), lambda f, b, be, nu_: (blk(b, nu_), 0)),
            pl.BlockSpec((None, None, D, tf), lambda f, b, be, nu_: (layer, be[blk(b, nu_)], 0, f)),
            pl.BlockSpec((None, None, D, tf), lambda f, b, be, nu_: (layer, be[blk(b, nu_)], 0, nf + f)),
        ],
        out_specs=pl.BlockSpec((bm, tf), lambda f, b, be, nu_: (b, f)),
        scratch_shapes=[pltpu.VMEM((D, tf), BF16), pltpu.VMEM((D, tf), BF16)],
    )
    vmem = 2 * bm * D * 2 + 4 * D * tf * 4 + 2 * D * tf * 2 + 2 * bm * tf * 2 + 6 * bm * tf * 4 + 4 * 2**20
    hidden = pl.pallas_call(
        _expert_up_kernel,
        grid_spec=up_spec,
        out_shape=jax.ShapeDtypeStruct((rows, F), BF16),
        compiler_params=_params(("arbitrary", "arbitrary"), vmem),
        name="moe_up",
    )(blk_e, nu, xs, w_gu, w_gu)

    down_spec = pltpu.PrefetchScalarGridSpec(
        num_scalar_prefetch=2,
        grid=(nd, nb),
        in_specs=[
            pl.BlockSpec((bm, F), lambda d, b, be, nu_: (blk(b, nu_), 0)),
            pl.BlockSpec((None, None, F, td), lambda d, b, be, nu_: (layer, be[blk(b, nu_)], 0, d)),
        ],
        out_specs=pl.BlockSpec((bm, td), lambda d, b, be, nu_: (b, d)),
        scratch_shapes=[pltpu.VMEM((F, td), BF16)],
    )
    vmem = 2 * bm * F * 2 + 2 * F * td * 4 + F * td * 2 + 2 * bm * td * 4 + 2 * bm * td * 4 + 4 * 2**20
    return pl.pallas_call(
        _expert_down_kernel,
        grid_spec=down_spec,
        out_shape=jax.ShapeDtypeStruct((rows, D), F32),
        compiler_params=_params(("arbitrary", "arbitrary"), vmem),
        name="moe_down",
    )(blk_e, nu, hidden, w_dn)


def _row_copy(src_hbm, src_row, dst, dst_row, sem):
    return pltpu.make_async_copy(src_hbm.at[pl.ds(src_row, 1)], dst.at[pl.ds(dst_row, 1)], sem)


def _combine_ln_kernel(pos_ref, x_ref, gt_ref, ys_hbm, g_ref, b_ref, o_ref, ob_ref, buf, sem, *, tc):
    def start(r, c):
        _row_copy(ys_hbm, pos_ref[0, r], buf.at[0], r, sem).start()
        _row_copy(ys_hbm, pos_ref[0, tc + r], buf.at[1], r, sem).start()
        return c

    lax.fori_loop(0, tc, start, 0, unroll=4)

    def wait(r, c):
        _row_copy(ys_hbm, 0, buf.at[0], r, sem).wait()
        _row_copy(ys_hbm, 0, buf.at[1], r, sem).wait()
        return c

    lax.fori_loop(0, tc, wait, 0, unroll=4)
    y = DEEPNORM_ALPHA * x_ref[...] + (buf[0] * gt_ref[:, 0:1] + buf[1] * gt_ref[:, 1:2])
    out = _layer_norm_rows(y, g_ref[...], b_ref[...])
    o_ref[...] = out
    ob_ref[...] = out.astype(BF16)


def _moe_combine_ln(x, ys, dest, gate, g, b):
    T, D = x.shape
    tc = _tile(T, 256, SUBLANES)
    nblk = T // tc
    pos = dest.reshape(2, nblk, tc).transpose(1, 0, 2).reshape(nblk, 1, 2 * tc)
    row = pl.BlockSpec((tc, D), lambda i: (i, 0))
    vec = pl.BlockSpec((1, D), lambda i: (0, 0))
    vmem = 2 * tc * D * 4 + 2 * tc * D * 4 + 2 * tc * D * (4 + 2) + 4 * tc * D * 4 + 4 * 2**20
    return pl.pallas_call(
        functools.partial(_combine_ln_kernel, tc=tc),
        grid=(nblk,),
        in_specs=[pl.BlockSpec((None, 1, 2 * tc), lambda i: (i, 0, 0), memory_space=pltpu.SMEM),
                  row, pl.BlockSpec((tc, 2), lambda i: (i, 0)), pl.BlockSpec(memory_space=pl.ANY), vec, vec],
        out_specs=[row, row],
        out_shape=[jax.ShapeDtypeStruct((T, D), F32), jax.ShapeDtypeStruct((T, D), BF16)],
        scratch_shapes=[pltpu.VMEM((2, tc, D), F32), pltpu.SemaphoreType.DMA(())],
        compiler_params=_params(("arbitrary",), vmem),
        name="moe_combine_ln",
    )(pos, x, gate.T, ys, g.reshape(1, D), b.reshape(1, D))


def _moe_layout(idx, T):
    E = N_EXPERTS
    bm = MOE_BM
    nb = -(-2 * T // bm) + E
    rows = nb * bm
    flat_e = idx.reshape(-1)
    tok = jnp.tile(jnp.arange(T, dtype=jnp.int32), 2)
    onehot = (flat_e[:, None] == jnp.arange(E, dtype=jnp.int32)[None, :]).astype(jnp.int32)
    csum = jnp.cumsum(onehot, axis=0)
    rank = jnp.sum(onehot * csum, axis=1) - 1
    counts = csum[-1]
    padded = (counts + bm - 1) // bm * bm
    pad_end = jnp.cumsum(padded)
    pad_start = pad_end - padded
    dest = (pad_start[flat_e] + rank).astype(jnp.int32)
    buf_tok = jnp.zeros((rows,), jnp.int32).at[dest].set(tok)
    blk_e = jnp.minimum(jnp.searchsorted(pad_end, jnp.arange(nb, dtype=jnp.int32) * bm, side="right"),
                        E - 1).astype(jnp.int32)
    n_used_rows = pad_end[-1].astype(jnp.int32)
    return dest, buf_tok, blk_e, n_used_rows, rows


def _routed_experts_ln(x, xb, xp, router_w, router_b, w_gu, w_dn, layer, g, b):
    T, D = x.shape
    idx, gate = _router(xb, router_w, router_b)
    dest, buf_tok, blk_e, n_used_rows, rows = _moe_layout(idx, T)
    xs = _moe_gather(xp, buf_tok, n_used_rows, rows, D)
    ys = _moe_experts(xs, w_gu, w_dn, layer, blk_e, n_used_rows // MOE_BM)
    return _moe_combine_ln(x, ys, dest, gate, g, b)


def kernel(x_prompt, x_sample, mem_prompt, mem_sample, da_w_qkv, da_lam, da_subln_g, da_w_o, sg_w_in, sg_b_in, sg_ln_g, sg_ln_b, sg_w_s, sg_b_s, sg_w_o, rg_w_in, rg_conv_w, rg_conv_b, rg_w_a, rg_b_a, rg_w_i, rg_b_i, rg_lambda, rg_w_o, ca_w_q, ca_w_kv, ca_w_o, router_w, router_b, moe_w_gu, moe_w_dn, ln_g, ln_b):
    Bp, S, D = x_prompt.shape
    Bs = x_sample.shape[0]
    assert x_sample.shape[1] == S, "both request groups must share the sequence length"
    B = Bp + Bs
    M = mem_prompt.shape[1]
    x = jnp.concatenate([x_prompt.reshape(Bp * S, D), x_sample.reshape(Bs * S, D)], axis=0)
    xb = x.astype(BF16)
    mem = jnp.concatenate([mem_prompt.reshape(Bp * M, D), mem_sample.reshape(Bs * M, D)], axis=0).astype(BF16)
    slopes = jnp.asarray([2.0 ** (-8.0 * (h + 1) / DA_HEADS) for h in range(DA_HEADS)], F32)
    dh = D // (2 * DA_HEADS)
    qkv_scale = jnp.concatenate([jnp.full((D,), LOG2E * dh ** -0.5, F32), jnp.ones((2 * D,), F32)])

    for i in range(DEPTH):
        kind, slot = i % N_MIXERS, i // N_MIXERS
        if kind == 0:
            lam_init = 0.8 - 0.6 * math.exp(-0.3 * i)
            lf = da_lam[slot].astype(F32)
            lam_full = (jnp.exp(jnp.sum(lf[0] * lf[1])) - jnp.exp(jnp.sum(lf[2] * lf[3])) + lam_init).reshape(1)
            qkv = _mm(xb, da_w_qkv, slot, col_scale=qkv_scale, out_dtype=BF16, name="da_qkv")
            o = _diff_attention(qkv, B, S, slopes, lam_full, da_subln_g[slot], lam_init)
            h = _mm(o, da_w_o, slot, name="da_out")
        elif kind == 1:
            z = _mm(xb, sg_w_in, slot, bias=sg_b_in[slot], act="gelu", out_dtype=BF16, name="sg_in")
            gated = _spatial_gate(z, sg_ln_g[slot], sg_ln_b[slot], sg_w_s[slot], sg_b_s[slot])
            h = _mm(gated, sg_w_o, slot, name="sg_out")
        else:
            xg = _mm(xb, rg_w_in, slot, name="rg_in")
            hr = _rglru_core(xg, B, S, rg_conv_w[slot], rg_conv_b[slot], rg_w_a[slot], rg_b_a[slot],
                             rg_w_i[slot], rg_b_i[slot], rg_lambda[slot])
            h = _mm(hr, rg_w_o, slot, name="rg_out")
        x, xb = _add_ln(x, h, ln_g[i, 0], ln_b[i, 0])

        q = _mm(xb, ca_w_q, i, out_dtype=BF16, name="ca_q")
        kv = _mm(mem, ca_w_kv, i, out_dtype=BF16, name="ca_kv")
        o = _memory_attention_core(q, kv, B, S, M)
        h = _mm(o, ca_w_o, i, name="ca_out")
        x, xb, xp = _add_ln(x, h, ln_g[i, 1], ln_b[i, 1], emit_packed=True)

        x, xb = _routed_experts_ln(x, xb, xp, router_w, router_b, moe_w_gu, moe_w_dn, i, ln_g[i, 2], ln_b[i, 2])

    y = x.reshape(B, S, D)
    return (y[:Bp], y[Bp:])
```

```python
import functools
import math

import jax
import jax.numpy as jnp
from jax import lax
from jax.experimental import pallas as pl
from jax.experimental.pallas import tpu as pltpu

DEPTH = 4
N_MIXERS = 3
DA_HEADS = 16
SG_CHUNK = 128
SG_GROUPS = 16
RG_BLOCKS = 16
RG_C = 8.0
CA_HEADS = 4
N_EXPERTS = 32
N_GROUPS = 8
LN_EPS = 1e-5
DEEPNORM_ALPHA = (2 * DEPTH) ** 0.25
LOG2E = math.log2(math.e)

V7X_VMEM_BYTES = 64 * 1024 * 1024
VMEM_CAP = V7X_VMEM_BYTES - 6 * 1024 * 1024
SUBLANES = 8
LANES = 128

MOE_BM = 512
MOE_TF = 256
MOE_TG = 256

BF16 = jnp.bfloat16
F32 = jnp.float32


def _tile(dim, pref, align):
    if dim <= pref:
        return dim
    t = (pref // align) * align
    while t > align and dim % t:
        t -= align
    assert dim % t == 0, (dim, pref, align)
    return t


def _params(sem, vmem_bytes):
    return pltpu.CompilerParams(dimension_semantics=sem,
                                vmem_limit_bytes=int(min(max(vmem_bytes, 16 * 2**20), VMEM_CAP)))


def _mm_kernel(*refs, act, has_bias, has_scale, cast_w):
    x_ref, w_ref = refs[0], refs[1]
    if cast_w:
        o_ref, wbf_ref = refs[-2], refs[-1]
        extra = list(refs[2:-2])

        @pl.when(pl.program_id(1) == 0)
        def _():
            wbf_ref[...] = w_ref[...].astype(BF16)
    else:
        o_ref, wbf_ref = refs[-1], w_ref
        extra = list(refs[2:-1])

    acc = jnp.dot(x_ref[...].astype(BF16), wbf_ref[...], preferred_element_type=F32)
    if has_bias:
        acc = acc + extra.pop(0)[...]
    if has_scale:
        acc = acc * extra.pop(0)[...]
    if act == "gelu":
        acc = jax.nn.gelu(acc)
    o_ref[...] = acc.astype(o_ref.dtype)


def _mm(x, w, layer, bias=None, col_scale=None, act=None, out_dtype=F32, name="mm"):
    M, K = x.shape
    _, K2, N = w.shape
    assert K == K2
    xb = x.dtype.itemsize
    wb = w.dtype.itemsize
    cast_w = w.dtype != BF16
    ob = jnp.dtype(out_dtype).itemsize
    tm = _tile(M, max(256, (8 * 2**20) // (K * xb)), 256 if M % 256 == 0 else SUBLANES)
    tn = _tile(N, max(256, (8 * 2**20) // (K * wb)), 256 if N % 256 == 0 else LANES)
    grid = (N // tn, M // tm)
    in_specs = [pl.BlockSpec((tm, K), lambda j, i: (i, 0)),
                pl.BlockSpec((None, K, tn), lambda j, i: (layer, 0, j))]
    args = [x, w]
    for vec in (bias, col_scale):
        if vec is not None:
            in_specs.append(pl.BlockSpec((1, tn), lambda j, i: (0, j)))
            args.append(vec.reshape(1, N).astype(F32))
    vmem = (2 * tm * K * xb + 2 * K * tn * wb + (K * tn * 2 if cast_w else 0) + 2 * tm * tn * ob
            + 2 * tm * tn * 4 + 4 * 2**20)
    return pl.pallas_call(
        functools.partial(_mm_kernel, act=act, has_bias=bias is not None, has_scale=col_scale is not None,
                          cast_w=cast_w),
        grid=grid,
        in_specs=in_specs,
        out_specs=pl.BlockSpec((tm, tn), lambda j, i: (i, j)),
        out_shape=jax.ShapeDtypeStruct((M, N), out_dtype),
        scratch_shapes=[pltpu.VMEM((K, tn), BF16)] if cast_w else [],
        compiler_params=_params(("parallel", "arbitrary"), vmem),
        name=name,
    )(*args)


def _layer_norm_rows(y, g, b):
    mu = jnp.mean(y, axis=-1, keepdims=True)
    yc = y - mu
    var = jnp.mean(yc * yc, axis=-1, keepdims=True)
    return yc * lax.rsqrt(var + LN_EPS) * g + b


def _pack_slabs(out, xp_ref):
    tm, D = out.shape
    half = D // 2
    ns = half // LANES
    lo = pltpu.bitcast(out[:, :half].astype(BF16).astype(F32), jnp.uint32) >> 16
    hi = pltpu.bitcast(out[:, half:].astype(BF16).astype(F32), jnp.uint32) & jnp.uint32(0xFFFF0000)
    w = hi | lo
    for c in range(ns):
        xp_ref[pl.ds(c, tm, stride=ns), :] = w[:, c * LANES:(c + 1) * LANES]


def _add_ln_kernel(x_ref, h_ref, g_ref, b_ref, o_ref, ob_ref, *maybe_xp_ref):
    y = DEEPNORM_ALPHA * x_ref[...] + h_ref[...].astype(F32)
    out = _layer_norm_rows(y, g_ref[...], b_ref[...])
    o_ref[...] = out
    ob_ref[...] = out.astype(BF16)
    if maybe_xp_ref:
        _pack_slabs(out, maybe_xp_ref[0])


def _add_ln(x, h, g, b, emit_packed=False):
    T, D = x.shape
    tm = _tile(T, 256, SUBLANES)
    ns = D // 2 // LANES
    row = pl.BlockSpec((tm, D), lambda i: (i, 0))
    vec = pl.BlockSpec((1, D), lambda i: (0, 0))
    out_specs = [row, row]
    out_shape = [jax.ShapeDtypeStruct((T, D), F32), jax.ShapeDtypeStruct((T, D), BF16)]
    if emit_packed:
        out_specs.append(pl.BlockSpec((tm * ns, LANES), lambda i: (i, 0)))
        out_shape.append(jax.ShapeDtypeStruct((T * ns, LANES), jnp.uint32))
    vmem = 2 * tm * D * (4 + h.dtype.itemsize + 4 + 2 + 2) + 6 * tm * D * 4
    return pl.pallas_call(
        _add_ln_kernel,
        grid=(T // tm,),
        in_specs=[row, row, vec, vec],
        out_specs=out_specs,
        out_shape=out_shape,
        compiler_params=_params(("parallel",), vmem),
        name="add_ln",
    )(x, h, g.reshape(1, D), b.reshape(1, D))


def _da_kernel(slope_ref, lam_ref, q_ref, k_ref, v_ref, g_ref, o_ref, bias_ref, *, tq, sub, lam_init):
    h = pl.program_id(0)
    qi = pl.program_id(1)
    S = k_ref.shape[0]
    dh = q_ref.shape[1] // 2

    @pl.when(pl.program_id(2) == 0)
    def _():
        qpos = qi * tq + lax.broadcasted_iota(jnp.int32, (tq, S), 0)
        kpos = lax.broadcasted_iota(jnp.int32, (tq, S), 1)
        bias_ref[...] = jnp.abs(qpos - kpos).astype(F32) * (-LOG2E * slope_ref[h])

    lam = lam_ref[0]
    k = k_ref[...]
    v = v_ref[...]

    def unnormalised(q_rows, rows, c):
        s = lax.dot_general(q_rows[:, c * dh:(c + 1) * dh], k[:, c * dh:(c + 1) * dh],
                            (((1,), (1,)), ((), ())), preferred_element_type=F32)
        xs = [s[:, j * LANES:(j + 1) * LANES] + bias_ref[rows, j * LANES:(j + 1) * LANES]
              for j in range(S // LANES)]
        m = functools.reduce(jnp.maximum, xs)
        m = jnp.max(m, axis=-1, keepdims=True)
        es = [jnp.exp2(x - m) for x in xs]
        l = jnp.sum(functools.reduce(lambda p, t: p + t, es), axis=-1, keepdims=True)
        e = jnp.concatenate([t.astype(BF16) for t in es], axis=1)
        return jnp.dot(e, v, preferred_element_type=F32), l

    for r0 in range(0, tq, sub):
        rows = slice(r0, r0 + sub)
        q_rows = q_ref[rows, :]
        pv1, l1 = unnormalised(q_rows, rows, 0)
        pv2, l2 = unnormalised(q_rows, rows, 1)
        o = pv1 * (1.0 / l1) - pv2 * (lam / l2)
        ms = jnp.mean(o * o, axis=-1, keepdims=True)
        o = o * lax.rsqrt(ms + LN_EPS) * g_ref[...] * (1.0 - lam_init)
        o_ref[rows, :] = o.astype(o_ref.dtype)


def _diff_attention(qkv, B, S, slopes, lam_full, subln_g, lam_init):
    T, D3 = qkv.shape
    D = D3 // 3
    H = DA_HEADS
    hw = D // H
    tq = _tile(S, 512, SUBLANES)
    nq = S // tq
    smem = pl.BlockSpec(memory_space=pltpu.SMEM)
    vmem = 10 * tq * S * 4 + 8 * S * hw * 2 + 4 * 2**20
    return pl.pallas_call(
        functools.partial(_da_kernel, tq=tq, sub=_tile(tq, 128, SUBLANES), lam_init=lam_init),
        grid=(H, nq, B),
        in_specs=[smem, smem,
                  pl.BlockSpec((tq, hw), lambda h, qi, b: (b * nq + qi, h)),
                  pl.BlockSpec((S, hw), lambda h, qi, b: (b, H + h)),
                  pl.BlockSpec((S, hw), lambda h, qi, b: (b, 2 * H + h)),
                  pl.BlockSpec((1, hw), lambda h, qi, b: (0, 0))],
        out_specs=pl.BlockSpec((tq, hw), lambda h, qi, b: (b * nq + qi, h)),
        out_shape=jax.ShapeDtypeStruct((T, D), BF16),
        scratch_shapes=[pltpu.VMEM((tq, S), F32)],
        compiler_params=_params(("parallel", "parallel", "arbitrary"), vmem),
        name="diff_attn",
    )(slopes, lam_full, qkv, qkv, qkv, subln_g.reshape(1, hw).astype(F32))


def _ca_kernel(q_ref, k_ref, v_ref, o_ref):
    dh = q_ref.shape[1]
    s = lax.dot_general(q_ref[...], k_ref[...], (((1,), (1,)), ((), ())),
                        preferred_element_type=F32) * (dh ** -0.5)
    m = jnp.max(s, axis=-1, keepdims=True)
    e = jnp.exp(s - m)
    p = e * (1.0 / jnp.sum(e, axis=-1, keepdims=True))
    o_ref[...] = jnp.dot(p.astype(BF16), v_ref[...], preferred_element_type=F32).astype(o_ref.dtype)


def _memory_attention_core(q, kv, B, S, M):
    T, D = q.shape
    H = CA_HEADS
    dh = D // H
    tq = _tile(S, 512, SUBLANES)
    nq = S // tq
    vmem = 4 * tq * dh * 2 * 2 + 4 * M * dh * 2 + 4 * tq * M * 4 + 2 * tq * dh * 4 + 4 * 2**20
    return pl.pallas_call(
        _ca_kernel,
        grid=(B, H, nq),
        in_specs=[pl.BlockSpec((tq, dh), lambda b, h, qi: (b * nq + qi, h)),
                  pl.BlockSpec((M, dh), lambda b, h, qi: (b, h)),
                  pl.BlockSpec((M, dh), lambda b, h, qi: (b, H + h))],
        out_specs=pl.BlockSpec((tq, dh), lambda b, h, qi: (b * nq + qi, h)),
        out_shape=jax.ShapeDtypeStruct((T, D), BF16),
        compiler_params=_params(("parallel", "parallel", "parallel"), vmem),
        name="mem_attn",
    )(q, kv, kv)


def _sg_kernel(u_ref, v_ref, g_ref, b_ref, ws_ref, bs_ref, o_ref, *, n_chunks):
    W = v_ref.shape[1]
    G = ws_ref.shape[0]
    gw = W // G
    vn = _layer_norm_rows(v_ref[...].astype(F32), g_ref[...], b_ref[...]).astype(BF16)
    for c in range(n_chunks):
        rows = slice(c * SG_CHUNK, (c + 1) * SG_CHUNK)
        for g in range(G):
            cols = slice(g * gw, (g + 1) * gw)
            y = jnp.dot(ws_ref[g], vn[rows, cols], preferred_element_type=F32) + bs_ref[:, g:g + 1]
            o_ref[rows, cols] = (u_ref[rows, cols].astype(F32) * y).astype(o_ref.dtype)


def _spatial_gate(z, ln_g, ln_b, w_s, b_s):
    T, W2 = z.shape
    W = W2 // 2
    n_chunks = 2 if T % (2 * SG_CHUNK) == 0 else 1
    R = n_chunks * SG_CHUNK
    G = w_s.shape[0]
    zb = z.dtype.itemsize
    vmem = 4 * R * W * zb + 2 * R * W * 2 + 4 * R * W * 4 + 4 * 2**20
    return pl.pallas_call(
        functools.partial(_sg_kernel, n_chunks=n_chunks),
        grid=(T // R,),
        in_specs=[pl.BlockSpec((R, W), lambda i: (i, 0)),
                  pl.BlockSpec((R, W), lambda i: (i, 1)),
                  pl.BlockSpec((1, W), lambda i: (0, 0)),
                  pl.BlockSpec((1, W), lambda i: (0, 0)),
                  pl.BlockSpec((G, SG_CHUNK, SG_CHUNK), lambda i: (0, 0, 0)),
                  pl.BlockSpec((SG_CHUNK, G), lambda i: (0, 0))],
        out_specs=pl.BlockSpec((R, W), lambda i: (i, 0)),
        out_shape=jax.ShapeDtypeStruct((T, W), BF16),
        compiler_params=_params(("parallel",), vmem),
        name="spatial_gate",
    )(z, z, ln_g.reshape(1, W), ln_b.reshape(1, W), w_s.astype(BF16), b_s.T.astype(F32))


def _rg_kernel(xr_ref, g_ref, cw_ref, cb_ref, w4_ref, b4_ref, c_ref, o_ref, a_s, b_s):
    S, C = xr_ref.shape
    xr = xr_ref[...].astype(F32)
    t = lax.broadcasted_iota(jnp.int32, (S, C), 0)
    x_m2 = jnp.where(t >= 2, pltpu.roll(xr, 2, 0), 0.0)
    x_m1 = jnp.where(t >= 1, pltpu.roll(xr, 1, 0), 0.0)
    x_p1 = jnp.where(t < S - 1, pltpu.roll(xr, S - 1, 0), 0.0)
    xc = x_m2 * cw_ref[0:1, :] + x_m1 * cw_ref[1:2, :] + xr * cw_ref[2:3, :] + x_p1 * cw_ref[3:4, :] + cb_ref[...]
    pre = jnp.dot(xc.astype(BF16), w4_ref[...], preferred_element_type=F32) + b4_ref[...]
    def sigmoid(z):
        return 0.5 * jnp.tanh(0.5 * z) + 0.5

    for d in range(2):
        r = sigmoid(pre[:, (2 * d) * C:(2 * d + 1) * C])
        i = sigmoid(pre[:, (2 * d + 1) * C:(2 * d + 2) * C])
        a = jnp.exp(c_ref[d:d + 1, :] * r)
        a_s[d] = a
        y = jnp.maximum(1.0 - a * a, 0.0)
        root = jnp.where(y > 0.0, y * lax.rsqrt(y), 0.0)
        b_s[d] = root * (i * xc)

    row = lax.broadcasted_iota(jnp.int32, (SUBLANES, C), 0)
    n_groups = S // SUBLANES

    def group_scan(A, Bv, reverse):
        for sh in (1, 2, 4):
            if reverse:
                ok = row < SUBLANES - sh
                rs = SUBLANES - sh
            else:
                ok = row >= sh
                rs = sh
            A_sh = jnp.where(ok, pltpu.roll(A, rs, 0), 1.0)
            B_sh = jnp.where(ok, pltpu.roll(Bv, rs, 0), 0.0)
            Bv = A * B_sh + Bv
            A = A * A_sh
        return A, Bv

    def body(gi, carry):
        hf, hb = carry
        r0 = pl.multiple_of(gi * SUBLANES, SUBLANES)
        A, Bv = group_scan(a_s[0, pl.ds(r0, SUBLANES), :], b_s[0, pl.ds(r0, SUBLANES), :], False)
        Hf = A * hf + Bv
        b_s[0, pl.ds(r0, SUBLANES), :] = Hf
        r1 = pl.multiple_of((n_groups - 1 - gi) * SUBLANES, SUBLANES)
        A, Bv = group_scan(a_s[1, pl.ds(r1, SUBLANES), :], b_s[1, pl.ds(r1, SUBLANES), :], True)
        Hb = A * hb + Bv
        b_s[1, pl.ds(r1, SUBLANES), :] = Hb
        return Hf[SUBLANES - 1:SUBLANES, :], Hb[0:1, :]

    zero = jnp.zeros((1, C), F32)
    lax.fori_loop(0, n_groups, body, (zero, zero), unroll=8)
    gate = jax.nn.gelu(g_ref[...].astype(F32))
    o_ref[...] = ((b_s[0] + b_s[1]) * gate).astype(o_ref.dtype)


def _rglru_core(xg, B, S, conv_w, conv_b, w_a, b_a, w_i, b_i, lam):
    T, R2 = xg.shape
    R = R2 // 2
    nb = RG_BLOCKS
    C = R // nb
    w4 = jnp.concatenate([w_a[0], w_i[0], w_a[1], w_i[1]], axis=-1).astype(BF16)
    b4 = jnp.stack([b_a[0].reshape(nb, C), b_i[0].reshape(nb, C), b_a[1].reshape(nb, C), b_i[1].reshape(nb, C)],
                   axis=1).reshape(nb, 1, 4 * C).astype(F32)
    coef = (-RG_C) * jax.nn.softplus(-lam.astype(F32))
    xb = xg.dtype.itemsize
    vmem = 4 * S * C * xb + 2 * S * C * 2 + 4 * S * C * 4 + 3 * S * 4 * C * 4 + 6 * S * C * 4 + 4 * 2**20
    return pl.pallas_call(
        _rg_kernel,
        grid=(B, nb),
        in_specs=[pl.BlockSpec((S, C), lambda b, n: (b, n)),
                  pl.BlockSpec((S, C), lambda b, n: (b, nb + n)),
                  pl.BlockSpec((4, C), lambda b, n: (0, n)),
                  pl.BlockSpec((1, C), lambda b, n: (0, n)),
                  pl.BlockSpec((None, C, 4 * C), lambda b, n: (n, 0, 0)),
                  pl.BlockSpec((None, 1, 4 * C), lambda b, n: (n, 0, 0)),
                  pl.BlockSpec((2, C), lambda b, n: (0, n))],
        out_specs=pl.BlockSpec((S, C), lambda b, n: (b, n)),
        out_shape=jax.ShapeDtypeStruct((T, R), BF16),
        scratch_shapes=[pltpu.VMEM((2, S, C), F32), pltpu.VMEM((2, S, C), F32)],
        compiler_params=_params(("parallel", "parallel"), vmem),
        name="rglru_core",
    )(xg, xg, conv_w.astype(F32), conv_b.reshape(1, R).astype(F32), w4, b4, coef)


def _first_top2(vals):
    m1 = vals[0]
    j1 = jnp.zeros(vals[0].shape, jnp.int32)
    for j in range(1, len(vals)):
        upd = vals[j] > m1
        m1 = jnp.where(upd, vals[j], m1)
        j1 = jnp.where(upd, j, j1)
    m2 = jnp.full(vals[0].shape, -jnp.inf, F32)
    j2 = jnp.zeros(vals[0].shape, jnp.int32)
    for j in range(len(vals)):
        upd = (j1 != j) & (vals[j] > m2)
        m2 = jnp.where(upd, vals[j], m2)
        j2 = jnp.where(upd, j, j2)
    return m1, j1, m2, j2


def _router_kernel(x_ref, rw_ref, rb_ref, idx_ref, gate_ref):
    G = N_GROUPS
    J = N_EXPERTS // G
    logits = lax.dot_general(rw_ref[...], x_ref[...], (((1,), (1,)), ((), ())), preferred_element_type=F32)
    s = jax.nn.sigmoid(logits)
    biased = s + rb_ref[...]
    a = [biased[j * G:(j + 1) * G, :] for j in range(J)]
    sv = [s[j * G:(j + 1) * G, :] for j in range(J)]
    m1, _, m2, _ = _first_top2(a)
    gscore = m1 + m2
    gidx = lax.broadcasted_iota(jnp.int32, gscore.shape, 0)
    gmax = jnp.max(gscore, axis=0, keepdims=True)
    gsel = jnp.min(jnp.where(gscore == gmax, gidx, G), axis=0, keepdims=True)
    mask = gidx == gsel
    av = [jnp.sum(jnp.where(mask, a[j], 0.0), axis=0, keepdims=True) for j in range(J)]
    sw = [jnp.sum(jnp.where(mask, sv[j], 0.0), axis=0, keepdims=True) for j in range(J)]
    _, j1, _, j2 = _first_top2(av)
    w1 = jnp.zeros_like(sw[0])
    w2 = jnp.zeros_like(sw[0])
    for j in range(J):
        w1 = jnp.where(j1 == j, sw[j], w1)
        w2 = jnp.where(j2 == j, sw[j], w2)
    tot = w1 + w2
    idx_ref[0:1, :] = gsel * J + j1
    idx_ref[1:2, :] = gsel * J + j2
    gate_ref[0:1, :] = w1 / tot
    gate_ref[1:2, :] = w2 / tot


def _router(xb, router_w, router_b):
    T, D = xb.shape
    G = N_GROUPS
    J = N_EXPERTS // G
    perm = jnp.asarray([(r % G) * J + r // G for r in range(N_EXPERTS)], jnp.int32)
    rw = router_w.T[perm].astype(BF16)
    rb = router_b[perm].reshape(N_EXPERTS, 1).astype(F32)
    tm = _tile(T, 1024, LANES)
    vmem = 2 * tm * D * 2 + 2 * N_EXPERTS * D * 2 + 64 * tm * 4 * 4 + 4 * 2**20
    return pl.pallas_call(
        _router_kernel,
        grid=(T // tm,),
        in_specs=[pl.BlockSpec((tm, D), lambda i: (i, 0)),
                  pl.BlockSpec((N_EXPERTS, D), lambda i: (0, 0)),
                  pl.BlockSpec((N_EXPERTS, 1), lambda i: (0, 0))],
        out_specs=[pl.BlockSpec((2, tm), lambda i: (0, i)), pl.BlockSpec((2, tm), lambda i: (0, i))],
        out_shape=[jax.ShapeDtypeStruct((2, T), jnp.int32), jax.ShapeDtypeStruct((2, T), F32)],
        compiler_params=_params(("parallel",), vmem),
        name="router",
    )(xb, rw, rb)


def _slab_copy(xp_hbm, tok, buf, slot, r, sem, ns, pitch):
    return pltpu.make_async_copy(xp_hbm.at[pl.ds(tok * ns, ns)], buf.at[slot, pl.ds(r * pitch, ns)], sem.at[slot])


def _gather_kernel(nused_ref, tok_ref, tok_next_ref, xp_hbm, o_ref, buf, sem, *, tg, ns, pitch):
    i = pl.program_id(0)
    nu = nused_ref[0]
    half = ns * LANES

    def issue(tref, slot):
        def body(r2, c):
            for p in range(2):
                r = 2 * r2 + p
                _slab_copy(xp_hbm, tref[0, r], buf, slot, r, sem, ns, pitch).start(priority=p)
            return c

        lax.fori_loop(0, tg // 2, body, 0, unroll=4)

    def consume(slot):
        def body(r, c):
            _slab_copy(xp_hbm, 0, buf, slot, r, sem, ns, pitch).wait()
            return c

        lax.fori_loop(0, tg, body, 0, unroll=8)
        for c in range(ns):
            w = buf[slot, pl.ds(c, tg, stride=pitch), :]
            lo = pltpu.bitcast(w << 16, F32)
            hi = pltpu.bitcast(w & jnp.uint32(0xFFFF0000), F32)
            o_ref[:, c * LANES:(c + 1) * LANES] = lo.astype(o_ref.dtype)
            o_ref[:, half + c * LANES:half + (c + 1) * LANES] = hi.astype(o_ref.dtype)

    for slot in range(2):
        @pl.when((i < nu) & (i % 2 == slot))
        def _(slot=slot):
            if slot == 0:
                @pl.when(i == 0)
                def _():
                    issue(tok_ref, 0)

            @pl.when(i + 1 < nu)
            def _():
                issue(tok_next_ref, 1 - slot)

            consume(slot)

    @pl.when(i >= nu)
    def _():
        o_ref[...] = jnp.zeros(o_ref.shape, o_ref.dtype)


def _moe_gather(xp, buf_tok, n_used_rows, rows, D):
    ns = D // 2 // LANES
    pitch = ns + SUBLANES
    tg = MOE_TG
    nblk = rows // tg
    tok3 = buf_tok.reshape(nblk, 1, tg)
    grid_spec = pltpu.PrefetchScalarGridSpec(
        num_scalar_prefetch=1,
        grid=(nblk,),
        in_specs=[pl.BlockSpec((None, 1, tg), lambda i, nu: (i, 0, 0), memory_space=pltpu.SMEM),
                  pl.BlockSpec((None, 1, tg), lambda i, nu: (jnp.minimum(i + 1, nblk - 1), 0, 0),
                               memory_space=pltpu.SMEM),
                  pl.BlockSpec(memory_space=pl.ANY)],
        out_specs=pl.BlockSpec((tg, D), lambda i, nu: (i, 0)),
        scratch_shapes=[pltpu.VMEM((2, tg * pitch, LANES), jnp.uint32), pltpu.SemaphoreType.DMA((2,))],
    )
    vmem = 2 * tg * pitch * LANES * 4 + 2 * tg * D * 2 + 4 * tg * D * 4 + 4 * 2**20
    return pl.pallas_call(
        functools.partial(_gather_kernel, tg=tg, ns=ns, pitch=pitch),
        grid_spec=grid_spec,
        out_shape=jax.ShapeDtypeStruct((rows, D), BF16),
        compiler_params=_params(("arbitrary",), vmem),
        name="moe_gather",
    )((n_used_rows // tg).reshape(1).astype(jnp.int32), tok3, tok3, xp)


def _expert_changed(be_ref, b):
    return (b == 0) | (be_ref[b] != be_ref[jnp.maximum(b - 1, 0)])


def _expert_up_kernel(be_ref, nu_ref, x_ref, wg_ref, wu_ref, h_ref, wgb, wub):
    b = pl.program_id(1)
    used = b < nu_ref[0]

    @pl.when(used & _expert_changed(be_ref, b))
    def _():
        wgb[...] = wg_ref[...].astype(BF16)
        wub[...] = wu_ref[...].astype(BF16)

    @pl.when(used)
    def _():
        x = x_ref[...]
        g = jnp.dot(x, wgb[...], preferred_element_type=F32)
        u = jnp.dot(x, wub[...], preferred_element_type=F32)
        h_ref[...] = (jax.nn.silu(g) * u).astype(h_ref.dtype)

    @pl.when(jnp.logical_not(used))
    def _():
        h_ref[...] = jnp.zeros(h_ref.shape, h_ref.dtype)


def _expert_down_kernel(be_ref, nu_ref, h_ref, wd_ref, o_ref, wdb):
    b = pl.program_id(1)
    used = b < nu_ref[0]

    @pl.when(used & _expert_changed(be_ref, b))
    def _():
        wdb[...] = wd_ref[...].astype(BF16)

    @pl.when(used)
    def _():
        acc = jnp.dot(h_ref[...], wdb[...], preferred_element_type=F32)
        half = acc.shape[1] // 2
        lo = pltpu.bitcast(acc[:, :half].astype(BF16).astype(F32), jnp.uint32) >> 16
        hi = pltpu.bitcast(acc[:, half:].astype(BF16).astype(F32), jnp.uint32) & jnp.uint32(0xFFFF0000)
        w = hi | lo
        for c in range(half // LANES):
            o_ref[:, c, :] = w[:, c * LANES:(c + 1) * LANES]

    @pl.when(jnp.logical_not(used))
    def _():
        o_ref[...] = jnp.zeros(o_ref.shape, o_ref.dtype)


def _moe_experts(xs, w_gu, w_dn, layer, blk_e, n_used_blocks):
    rows, D = xs.shape
    F = w_gu.shape[-1] // 2
    bm = MOE_BM
    nb = rows // bm
    tf = _tile(F, MOE_TF, LANES)
    nf = F // tf
    td = _tile(D, max(256, (8 * 2**20) // (F * 4)), 256 if D % 256 == 0 else LANES)
    nd = D // td
    sl = td // 2 // LANES
    nu = n_used_blocks.reshape(1).astype(jnp.int32)

    def blk(b, nu_ref):
        return jnp.minimum(b, nu_ref[0] - 1)

    up_spec = pltpu.PrefetchScalarGridSpec(
        num_scalar_prefetch=2,
        grid=(nf, nb),
        in_specs=[
            pl.BlockSpec((bm, D), lambda f, b, be, nu_: (blk(b, nu_), 0)),
            pl.BlockSpec((None, None, D, tf), lambda f, b, be, nu_: (layer, be[blk(b, nu_)], 0, f)),
            pl.BlockSpec((None, None, D, tf), lambda f, b, be, nu_: (layer, be[blk(b, nu_)], 0, nf + f)),
        ],
        out_specs=pl.BlockSpec((bm, tf), lambda f, b, be, nu_: (b, f)),
        scratch_shapes=[pltpu.VMEM((D, tf), BF16), pltpu.VMEM((D, tf), BF16)],
    )
    vmem = 2 * bm * D * 2 + 4 * D * tf * 4 + 2 * D * tf * 2 + 2 * bm * tf * 2 + 6 * bm * tf * 4 + 4 * 2**20
    hidden = pl.pallas_call(
        _expert_up_kernel,
        grid_spec=up_spec,
        out_shape=jax.ShapeDtypeStruct((rows, F), BF16),
        compiler_params=_params(("arbitrary", "arbitrary"), vmem),
        name="moe_up",
    )(blk_e, nu, xs, w_gu, w_gu)

    down_spec = pltpu.PrefetchScalarGridSpec(
        num_scalar_prefetch=2,
        grid=(nd, nb),
        in_specs=[
            pl.BlockSpec((bm, F), lambda d, b, be, nu_: (blk(b, nu_), 0)),
            pl.BlockSpec((None, None, F, td), lambda d, b, be, nu_: (layer, be[blk(b, nu_)], 0, d)),
        ],
        out_specs=pl.BlockSpec((bm, None, sl, LANES), lambda d, b, be, nu_: (b, d, 0, 0)),
        scratch_shapes=[pltpu.VMEM((F, td), BF16)],
    )
    vmem = 2 * bm * F * 2 + 2 * F * td * 4 + F * td * 2 + 2 * bm * td * 2 + 4 * bm * td * 4 + 4 * 2**20
    ys = pl.pallas_call(
        _expert_down_kernel,
        grid_spec=down_spec,
        out_shape=jax.ShapeDtypeStruct((rows, nd, sl, LANES), jnp.uint32),
        compiler_params=_params(("arbitrary", "arbitrary"), vmem),
        name="moe_down",
    )(blk_e, nu, hidden, w_dn)
    return ys.reshape(rows * nd * sl, LANES), td


def _combine_ln_kernel(pos_ref, pos_next_ref, x_ref, gt_ref, ys_hbm, g_ref, b_ref, o_ref, ob_ref, buf, sem,
                       *, tc, ns, pitch, td):
    i = pl.program_id(0)
    n = pl.num_programs(0)
    sl = td // 2 // LANES

    def copy(k, pos, slot, r):
        return pltpu.make_async_copy(ys_hbm.at[pl.ds(pos * ns, ns)], buf.at[slot, k, pl.ds(r * pitch, ns)],
                                     sem.at[slot])

    def issue(pref, slot):
        def body(r, c):
            for k in range(2):
                copy(k, pref[0, k * tc + r], slot, r).start(priority=k)
            return c

        lax.fori_loop(0, tc, body, 0, unroll=4)

    def consume(slot):
        def body(r, c):
            for k in range(2):
                copy(k, 0, slot, r).wait()
            return c

        lax.fori_loop(0, tc, body, 0, unroll=4)
        g0 = gt_ref[:, 0:1]
        g1 = gt_ref[:, 1:2]
        tiles = [None] * (2 * ns)
        for s in range(ns):
            d, c = divmod(s, sl)
            w0 = buf[slot, 0, pl.ds(s, tc, stride=pitch), :]
            w1 = buf[slot, 1, pl.ds(s, tc, stride=pitch), :]
            lo = pltpu.bitcast(w0 << 16, F32) * g0 + pltpu.bitcast(w1 << 16, F32) * g1
            hi = (pltpu.bitcast(w0 & jnp.uint32(0xFFFF0000), F32) * g0
                  + pltpu.bitcast(w1 & jnp.uint32(0xFFFF0000), F32) * g1)
            tiles[d * 2 * sl + c] = lo
            tiles[d * 2 * sl + sl + c] = hi
        y = DEEPNORM_ALPHA * x_ref[...] + jnp.concatenate(tiles, axis=1)
        out = _layer_norm_rows(y, g_ref[...], b_ref[...])
        o_ref[...] = out
        ob_ref[...] = out.astype(BF16)

    for slot in range(2):
        @pl.when(i % 2 == slot)
        def _(slot=slot):
            if slot == 0:
                @pl.when(i == 0)
                def _():
                    issue(pos_ref, 0)

            @pl.when(i + 1 < n)
            def _():
                issue(pos_next_ref, 1 - slot)

            consume(slot)


def _moe_combine_ln(x, ys, td, dest, gate, g, b):
    T, D = x.shape
    ns = D // 2 // LANES
    pitch = ns + SUBLANES
    tc = _tile(T, 256, SUBLANES)
    nblk = T // tc
    pos = dest.reshape(2, nblk, tc).transpose(1, 0, 2).reshape(nblk, 1, 2 * tc)
    row = pl.BlockSpec((tc, D), lambda i: (i, 0))
    vec = pl.BlockSpec((1, D), lambda i: (0, 0))
    vmem = 4 * tc * pitch * LANES * 4 + 2 * tc * D * 4 + 2 * tc * D * (4 + 2) + 6 * tc * D * 4 + 4 * 2**20
    return pl.pallas_call(
        functools.partial(_combine_ln_kernel, tc=tc, ns=ns, pitch=pitch, td=td),
        grid=(nblk,),
        in_specs=[pl.BlockSpec((None, 1, 2 * tc), lambda i: (i, 0, 0), memory_space=pltpu.SMEM),
                  pl.BlockSpec((None, 1, 2 * tc), lambda i: (jnp.minimum(i + 1, nblk - 1), 0, 0),
                               memory_space=pltpu.SMEM),
                  row, pl.BlockSpec((tc, 2), lambda i: (i, 0)), pl.BlockSpec(memory_space=pl.ANY), vec, vec],
        out_specs=[row, row],
        out_shape=[jax.ShapeDtypeStruct((T, D), F32), jax.ShapeDtypeStruct((T, D), BF16)],
        scratch_shapes=[pltpu.VMEM((2, 2, tc * pitch, LANES), jnp.uint32), pltpu.SemaphoreType.DMA((2,))],
        compiler_params=_params(("arbitrary",), vmem),
        name="moe_combine_ln",
    )(pos, pos, x, gate.T, ys, g.reshape(1, D), b.reshape(1, D))


def _moe_layout(idx, T):
    E = N_EXPERTS
    bm = MOE_BM
    nb = -(-2 * T // bm) + E
    rows = nb * bm
    flat_e = idx.reshape(-1)
    tok = jnp.tile(jnp.arange(T, dtype=jnp.int32), 2)
    onehot = (flat_e[:, None] == jnp.arange(E, dtype=jnp.int32)[None, :]).astype(jnp.int32)
    csum = jnp.cumsum(onehot, axis=0)
    rank = jnp.sum(onehot * csum, axis=1) - 1
    counts = csum[-1]
    padded = (counts + bm - 1) // bm * bm
    pad_end = jnp.cumsum(padded)
    pad_start = pad_end - padded
    dest = (pad_start[flat_e] + rank).astype(jnp.int32)
    buf_tok = jnp.zeros((rows,), jnp.int32).at[dest].set(tok)
    blk_e = jnp.minimum(jnp.searchsorted(pad_end, jnp.arange(nb, dtype=jnp.int32) * bm, side="right"),
                        E - 1).astype(jnp.int32)
    n_used_rows = pad_end[-1].astype(jnp.int32)
    return dest, buf_tok, blk_e, n_used_rows, rows


def _routed_experts_ln(x, xb, xp, router_w, router_b, w_gu, w_dn, layer, g, b):
    T, D = x.shape
    idx, gate = _router(xb, router_w, router_b)
    dest, buf_tok, blk_e, n_used_rows, rows = _moe_layout(idx, T)
    xs = _moe_gather(xp, buf_tok, n_used_rows, rows, D)
    ys, td = _moe_experts(xs, w_gu, w_dn, layer, blk_e, n_used_rows // MOE_BM)
    return _moe_combine_ln(x, ys, td, dest, gate, g, b)


def kernel(x_prompt, x_sample, mem_prompt, mem_sample, da_w_qkv, da_lam, da_subln_g, da_w_o, sg_w_in, sg_b_in, sg_ln_g, sg_ln_b, sg_w_s, sg_b_s, sg_w_o, rg_w_in, rg_conv_w, rg_conv_b, rg_w_a, rg_b_a, rg_w_i, rg_b_i, rg_lambda, rg_w_o, ca_w_q, ca_w_kv, ca_w_o, router_w, router_b, moe_w_gu, moe_w_dn, ln_g, ln_b):
    Bp, S, D = x_prompt.shape
    Bs = x_sample.shape[0]
    assert x_sample.shape[1] == S, "both request groups must share the sequence length"
    B = Bp + Bs
    M = mem_prompt.shape[1]
    x = jnp.concatenate([x_prompt.reshape(Bp * S, D), x_sample.reshape(Bs * S, D)], axis=0)
    xb = x.astype(BF16)
    mem = jnp.concatenate([mem_prompt.reshape(Bp * M, D), mem_sample.reshape(Bs * M, D)], axis=0).astype(BF16)
    slopes = jnp.asarray([2.0 ** (-8.0 * (h + 1) / DA_HEADS) for h in range(DA_HEADS)], F32)
    dh = D // (2 * DA_HEADS)
    qkv_scale = jnp.concatenate([jnp.full((D,), LOG2E * dh ** -0.5, F32), jnp.ones((2 * D,), F32)])

    for i in range(DEPTH):
        kind, slot = i % N_MIXERS, i // N_MIXERS
        if kind == 0:
            lam_init = 0.8 - 0.6 * math.exp(-0.3 * i)
            lf = da_lam[slot].astype(F32)
            lam_full = (jnp.exp(jnp.sum(lf[0] * lf[1])) - jnp.exp(jnp.sum(lf[2] * lf[3])) + lam_init).reshape(1)
            qkv = _mm(xb, da_w_qkv, slot, col_scale=qkv_scale, out_dtype=BF16, name="da_qkv")
            o = _diff_attention(qkv, B, S, slopes, lam_full, da_subln_g[slot], lam_init)
            h = _mm(o, da_w_o, slot, name="da_out")
        elif kind == 1:
            z = _mm(xb, sg_w_in.astype(BF16), slot, bias=sg_b_in[slot], act="gelu", out_dtype=BF16, name="sg_in")
            gated = _spatial_gate(z, sg_ln_g[slot], sg_ln_b[slot], sg_w_s[slot], sg_b_s[slot])
            h = _mm(gated, sg_w_o.astype(BF16), slot, name="sg_out")
        else:
            xg = _mm(xb, rg_w_in, slot, name="rg_in")
            hr = _rglru_core(xg, B, S, rg_conv_w[slot], rg_conv_b[slot], rg_w_a[slot], rg_b_a[slot],
                             rg_w_i[slot], rg_b_i[slot], rg_lambda[slot])
            h = _mm(hr, rg_w_o, slot, name="rg_out")
        x, xb = _add_ln(x, h, ln_g[i, 0], ln_b[i, 0])

        q = _mm(xb, ca_w_q, i, out_dtype=BF16, name="ca_q")
        kv = _mm(mem, ca_w_kv, i, out_dtype=BF16, name="ca_kv")
        o = _memory_attention_core(q, kv, B, S, M)
        h = _mm(o, ca_w_o, i, name="ca_out")
        x, xb, xp = _add_ln(x, h, ln_g[i, 1], ln_b[i, 1], emit_packed=True)

        x, xb = _routed_experts_ln(x, xb, xp, router_w, router_b, moe_w_gu, moe_w_dn, i, ln_g[i, 2], ln_b[i, 2])

    y = x.reshape(B, S, D)
    return (y[:Bp], y[Bp:])
```

```python
import functools
import math

import jax
import jax.numpy as jnp
from jax import lax
from jax.experimental import pallas as pl
from jax.experimental.pallas import tpu as pltpu

DEPTH = 4
N_MIXERS = 3
DA_HEADS = 16
SG_CHUNK = 128
SG_GROUPS = 16
RG_BLOCKS = 16
RG_C = 8.0
CA_HEADS = 4
N_EXPERTS = 32
N_GROUPS = 8
LN_EPS = 1e-5
DEEPNORM_ALPHA = (2 * DEPTH) ** 0.25
LOG2E = math.log2(math.e)

V7X_VMEM_BYTES = 64 * 1024 * 1024
VMEM_CAP = V7X_VMEM_BYTES - 6 * 1024 * 1024
SUBLANES = 8
LANES = 128

MOE_BM = 512
MOE_SUB = 128
MOE_TF = 512
MOE_TG = 256

BF16 = jnp.bfloat16
F32 = jnp.float32


def _tile(dim, pref, align):
    if dim <= pref:
        return dim
    t = (pref // align) * align
    while t > align and dim % t:
        t -= align
    assert dim % t == 0, (dim, pref, align)
    return t


def _params(sem, vmem_bytes):
    return pltpu.CompilerParams(dimension_semantics=sem,
                                vmem_limit_bytes=int(min(max(vmem_bytes, 16 * 2**20), VMEM_CAP)))


def _mm_kernel(*refs, act, has_bias, has_scale, cast_w):
    x_ref, w_ref = refs[0], refs[1]
    if cast_w:
        o_ref, wbf_ref = refs[-2], refs[-1]
        extra = list(refs[2:-2])

        @pl.when(pl.program_id(1) == 0)
        def _():
            wbf_ref[...] = w_ref[...].astype(BF16)
    else:
        o_ref, wbf_ref = refs[-1], w_ref
        extra = list(refs[2:-1])

    acc = jnp.dot(x_ref[...].astype(BF16), wbf_ref[...], preferred_element_type=F32)
    if has_bias:
        acc = acc + extra.pop(0)[...]
    if has_scale:
        acc = acc * extra.pop(0)[...]
    if act == "gelu":
        acc = jax.nn.gelu(acc)
    o_ref[...] = acc.astype(o_ref.dtype)


def _mm(x, w, layer, bias=None, col_scale=None, act=None, out_dtype=F32, name="mm"):
    M, K = x.shape
    _, K2, N = w.shape
    assert K == K2
    xb = x.dtype.itemsize
    wb = w.dtype.itemsize
    cast_w = w.dtype != BF16
    ob = jnp.dtype(out_dtype).itemsize
    tm = _tile(M, max(256, (8 * 2**20) // (K * xb)), 256 if M % 256 == 0 else SUBLANES)
    tn = _tile(N, max(256, (8 * 2**20) // (K * wb)), 256 if N % 256 == 0 else LANES)
    grid = (N // tn, M // tm)
    in_specs = [pl.BlockSpec((tm, K), lambda j, i: (i, 0)),
                pl.BlockSpec((None, K, tn), lambda j, i: (layer, 0, j))]
    args = [x, w]
    for vec in (bias, col_scale):
        if vec is not None:
            in_specs.append(pl.BlockSpec((1, tn), lambda j, i: (0, j)))
            args.append(vec.reshape(1, N).astype(F32))
    vmem = (2 * tm * K * xb + 2 * K * tn * wb + (K * tn * 2 if cast_w else 0) + 2 * tm * tn * ob
            + 2 * tm * tn * 4 + 4 * 2**20)
    return pl.pallas_call(
        functools.partial(_mm_kernel, act=act, has_bias=bias is not None, has_scale=col_scale is not None,
                          cast_w=cast_w),
        grid=grid,
        in_specs=in_specs,
        out_specs=pl.BlockSpec((tm, tn), lambda j, i: (i, j)),
        out_shape=jax.ShapeDtypeStruct((M, N), out_dtype),
        scratch_shapes=[pltpu.VMEM((K, tn), BF16)] if cast_w else [],
        compiler_params=_params(("parallel", "arbitrary"), vmem),
        name=name,
    )(*args)


def _layer_norm_rows(y, g, b):
    mu = jnp.mean(y, axis=-1, keepdims=True)
    yc = y - mu
    var = jnp.mean(yc * yc, axis=-1, keepdims=True)
    return yc * lax.rsqrt(var + LN_EPS) * g + b


def _pack_slabs(out, xp_ref):
    tm, D = out.shape
    half = D // 2
    ns = half // LANES
    lo = pltpu.bitcast(out[:, :half].astype(BF16).astype(F32), jnp.uint32) >> 16
    hi = pltpu.bitcast(out[:, half:].astype(BF16).astype(F32), jnp.uint32) & jnp.uint32(0xFFFF0000)
    w = hi | lo
    for c in range(ns):
        xp_ref[pl.ds(c, tm, stride=ns), :] = w[:, c * LANES:(c + 1) * LANES]


def _add_ln_kernel(x_ref, h_ref, g_ref, b_ref, o_ref, ob_ref, *maybe_xp_ref):
    y = DEEPNORM_ALPHA * x_ref[...] + h_ref[...].astype(F32)
    out = _layer_norm_rows(y, g_ref[...], b_ref[...])
    o_ref[...] = out
    ob_ref[...] = out.astype(BF16)
    if maybe_xp_ref:
        _pack_slabs(out, maybe_xp_ref[0])


def _add_ln(x, h, g, b, emit_packed=False):
    T, D = x.shape
    tm = _tile(T, 256, SUBLANES)
    ns = D // 2 // LANES
    row = pl.BlockSpec((tm, D), lambda i: (i, 0))
    vec = pl.BlockSpec((1, D), lambda i: (0, 0))
    out_specs = [row, row]
    out_shape = [jax.ShapeDtypeStruct((T, D), F32), jax.ShapeDtypeStruct((T, D), BF16)]
    if emit_packed:
        out_specs.append(pl.BlockSpec((tm * ns, LANES), lambda i: (i, 0)))
        out_shape.append(jax.ShapeDtypeStruct((T * ns, LANES), jnp.uint32))
    vmem = 2 * tm * D * (4 + h.dtype.itemsize + 4 + 2 + 2) + 6 * tm * D * 4
    return pl.pallas_call(
        _add_ln_kernel,
        grid=(T // tm,),
        in_specs=[row, row, vec, vec],
        out_specs=out_specs,
        out_shape=out_shape,
        compiler_params=_params(("parallel",), vmem),
        name="add_ln",
    )(x, h, g.reshape(1, D), b.reshape(1, D))


def _da_kernel(slope_ref, lam_ref, q_ref, k_ref, v_ref, g_ref, o_ref, bias_ref, *, tq, sub, lam_init):
    h = pl.program_id(0)
    qi = pl.program_id(1)
    S = k_ref.shape[0]
    dh = q_ref.shape[1] // 2

    @pl.when(pl.program_id(2) == 0)
    def _():
        qpos = qi * tq + lax.broadcasted_iota(jnp.int32, (tq, S), 0)
        kpos = lax.broadcasted_iota(jnp.int32, (tq, S), 1)
        bias_ref[...] = jnp.abs(qpos - kpos).astype(F32) * (-LOG2E * slope_ref[h])

    lam = lam_ref[0]
    k = k_ref[...]
    v = v_ref[...]

    def unnormalised(q_rows, rows, c):
        s = lax.dot_general(q_rows[:, c * dh:(c + 1) * dh], k[:, c * dh:(c + 1) * dh],
                            (((1,), (1,)), ((), ())), preferred_element_type=F32)
        xs = [s[:, j * LANES:(j + 1) * LANES] + bias_ref[rows, j * LANES:(j + 1) * LANES]
              for j in range(S // LANES)]
        m = functools.reduce(jnp.maximum, xs)
        m = jnp.max(m, axis=-1, keepdims=True)
        es = [jnp.exp2(x - m) for x in xs]
        l = jnp.sum(functools.reduce(lambda p, t: p + t, es), axis=-1, keepdims=True)
        e = jnp.concatenate([t.astype(BF16) for t in es], axis=1)
        return jnp.dot(e, v, preferred_element_type=F32), l

    for r0 in range(0, tq, sub):
        rows = slice(r0, r0 + sub)
        q_rows = q_ref[rows, :]
        pv1, l1 = unnormalised(q_rows, rows, 0)
        pv2, l2 = unnormalised(q_rows, rows, 1)
        o = pv1 * (1.0 / l1) - pv2 * (lam / l2)
        ms = jnp.mean(o * o, axis=-1, keepdims=True)
        o = o * lax.rsqrt(ms + LN_EPS) * g_ref[...] * (1.0 - lam_init)
        o_ref[rows, :] = o.astype(o_ref.dtype)


def _diff_attention(qkv, B, S, slopes, lam_full, subln_g, lam_init):
    T, D3 = qkv.shape
    D = D3 // 3
    H = DA_HEADS
    hw = D // H
    tq = _tile(S, 512, SUBLANES)
    nq = S // tq
    smem = pl.BlockSpec(memory_space=pltpu.SMEM)
    vmem = 10 * tq * S * 4 + 8 * S * hw * 2 + 4 * 2**20
    return pl.pallas_call(
        functools.partial(_da_kernel, tq=tq, sub=_tile(tq, 128, SUBLANES), lam_init=lam_init),
        grid=(H, nq, B),
        in_specs=[smem, smem,
                  pl.BlockSpec((tq, hw), lambda h, qi, b: (b * nq + qi, h)),
                  pl.BlockSpec((S, hw), lambda h, qi, b: (b, H + h)),
                  pl.BlockSpec((S, hw), lambda h, qi, b: (b, 2 * H + h)),
                  pl.BlockSpec((1, hw), lambda h, qi, b: (0, 0))],
        out_specs=pl.BlockSpec((tq, hw), lambda h, qi, b: (b * nq + qi, h)),
        out_shape=jax.ShapeDtypeStruct((T, D), BF16),
        scratch_shapes=[pltpu.VMEM((tq, S), F32)],
        compiler_params=_params(("parallel", "parallel", "arbitrary"), vmem),
        name="diff_attn",
    )(slopes, lam_full, qkv, qkv, qkv, subln_g.reshape(1, hw).astype(F32))


def _ca_kernel(q_ref, k_ref, v_ref, o_ref):
    dh = q_ref.shape[1]
    s = lax.dot_general(q_ref[...], k_ref[...], (((1,), (1,)), ((), ())),
                        preferred_element_type=F32) * (dh ** -0.5)
    m = jnp.max(s, axis=-1, keepdims=True)
    e = jnp.exp(s - m)
    p = e * (1.0 / jnp.sum(e, axis=-1, keepdims=True))
    o_ref[...] = jnp.dot(p.astype(BF16), v_ref[...], preferred_element_type=F32).astype(o_ref.dtype)


def _memory_attention_core(q, kv, B, S, M):
    T, D = q.shape
    H = CA_HEADS
    dh = D // H
    tq = _tile(S, 512, SUBLANES)
    nq = S // tq
    vmem = 4 * tq * dh * 2 * 2 + 4 * M * dh * 2 + 4 * tq * M * 4 + 2 * tq * dh * 4 + 4 * 2**20
    return pl.pallas_call(
        _ca_kernel,
        grid=(B, H, nq),
        in_specs=[pl.BlockSpec((tq, dh), lambda b, h, qi: (b * nq + qi, h)),
                  pl.BlockSpec((M, dh), lambda b, h, qi: (b, h)),
                  pl.BlockSpec((M, dh), lambda b, h, qi: (b, H + h))],
        out_specs=pl.BlockSpec((tq, dh), lambda b, h, qi: (b * nq + qi, h)),
        out_shape=jax.ShapeDtypeStruct((T, D), BF16),
        compiler_params=_params(("parallel", "parallel", "parallel"), vmem),
        name="mem_attn",
    )(q, kv, kv)


def _sg_kernel(u_ref, v_ref, g_ref, b_ref, ws_ref, bs_ref, o_ref, *, n_chunks):
    W = v_ref.shape[1]
    G = ws_ref.shape[0]
    gw = W // G
    vn = _layer_norm_rows(v_ref[...].astype(F32), g_ref[...], b_ref[...]).astype(BF16)
    for c in range(n_chunks):
        rows = slice(c * SG_CHUNK, (c + 1) * SG_CHUNK)
        for g in range(G):
            cols = slice(g * gw, (g + 1) * gw)
            y = jnp.dot(ws_ref[g], vn[rows, cols], preferred_element_type=F32) + bs_ref[:, g:g + 1]
            o_ref[rows, cols] = (u_ref[rows, cols].astype(F32) * y).astype(o_ref.dtype)


def _spatial_gate(z, ln_g, ln_b, w_s, b_s):
    T, W2 = z.shape
    W = W2 // 2
    n_chunks = 2 if T % (2 * SG_CHUNK) == 0 else 1
    R = n_chunks * SG_CHUNK
    G = w_s.shape[0]
    zb = z.dtype.itemsize
    vmem = 4 * R * W * zb + 2 * R * W * 2 + 4 * R * W * 4 + 4 * 2**20
    return pl.pallas_call(
        functools.partial(_sg_kernel, n_chunks=n_chunks),
        grid=(T // R,),
        in_specs=[pl.BlockSpec((R, W), lambda i: (i, 0)),
                  pl.BlockSpec((R, W), lambda i: (i, 1)),
                  pl.BlockSpec((1, W), lambda i: (0, 0)),
                  pl.BlockSpec((1, W), lambda i: (0, 0)),
                  pl.BlockSpec((G, SG_CHUNK, SG_CHUNK), lambda i: (0, 0, 0)),
                  pl.BlockSpec((SG_CHUNK, G), lambda i: (0, 0))],
        out_specs=pl.BlockSpec((R, W), lambda i: (i, 0)),
        out_shape=jax.ShapeDtypeStruct((T, W), BF16),
        compiler_params=_params(("parallel",), vmem),
        name="spatial_gate",
    )(z, z, ln_g.reshape(1, W), ln_b.reshape(1, W), w_s.astype(BF16), b_s.T.astype(F32))


def _rg_kernel(xr_ref, g_ref, cw_ref, cb_ref, w4_ref, b4_ref, c_ref, o_ref, a_s, b_s, h_s):
    S, C = xr_ref.shape
    xr = xr_ref[...].astype(F32)
    t = lax.broadcasted_iota(jnp.int32, (S, C), 0)
    x_m2 = jnp.where(t >= 2, pltpu.roll(xr, 2, 0), 0.0)
    x_m1 = jnp.where(t >= 1, pltpu.roll(xr, 1, 0), 0.0)
    x_p1 = jnp.where(t < S - 1, pltpu.roll(xr, S - 1, 0), 0.0)
    xc = x_m2 * cw_ref[0:1, :] + x_m1 * cw_ref[1:2, :] + xr * cw_ref[2:3, :] + x_p1 * cw_ref[3:4, :] + cb_ref[...]
    pre = jnp.dot(xc.astype(BF16), w4_ref[...], preferred_element_type=F32) + b4_ref[...]
    def sigmoid(z):
        return 0.5 * jnp.tanh(0.5 * z) + 0.5

    for d in range(2):
        r = sigmoid(pre[:, (2 * d) * C:(2 * d + 1) * C])
        i = sigmoid(pre[:, (2 * d + 1) * C:(2 * d + 2) * C])
        a = jnp.exp(c_ref[d:d + 1, :] * r)
        a_s[d] = a
        y = jnp.maximum(1.0 - a * a, 0.0)
        root = jnp.where(y > 0.0, y * lax.rsqrt(y), 0.0)
        b_s[d] = root * (i * xc)

    row = lax.broadcasted_iota(jnp.int32, (SUBLANES, C), 0)
    n_groups = S // SUBLANES

    def group_scan(A, Bv, reverse):
        for sh in (1, 2, 4):
            if reverse:
                ok = row < SUBLANES - sh
                rs = SUBLANES - sh
            else:
                ok = row >= sh
                rs = sh
            A_sh = jnp.where(ok, pltpu.roll(A, rs, 0), 1.0)
            B_sh = jnp.where(ok, pltpu.roll(Bv, rs, 0), 0.0)
            Bv = A * B_sh + Bv
            A = A * A_sh
        return A, Bv

    def body(gi, carry):
        hf, hb = carry
        r0 = pl.multiple_of(gi * SUBLANES, SUBLANES)
        A, Bv = group_scan(a_s[0, pl.ds(r0, SUBLANES), :], b_s[0, pl.ds(r0, SUBLANES), :], False)
        Hf = A * hf + Bv
        h_s[0, pl.ds(r0, SUBLANES), :] = Hf
        r1 = pl.multiple_of((n_groups - 1 - gi) * SUBLANES, SUBLANES)
        A, Bv = group_scan(a_s[1, pl.ds(r1, SUBLANES), :], b_s[1, pl.ds(r1, SUBLANES), :], True)
        Hb = A * hb + Bv
        h_s[1, pl.ds(r1, SUBLANES), :] = Hb
        return Hf[SUBLANES - 1:SUBLANES, :], Hb[0:1, :]

    zero = jnp.zeros((1, C), F32)
    lax.fori_loop(0, n_groups, body, (zero, zero), unroll=8)
    gate = jax.nn.gelu(g_ref[...].astype(F32))
    o_ref[...] = ((h_s[0] + h_s[1]) * gate).astype(o_ref.dtype)


def _rglru_core(xg, B, S, conv_w, conv_b, w_a, b_a, w_i, b_i, lam):
    T, R2 = xg.shape
    R = R2 // 2
    nb = RG_BLOCKS
    C = R // nb
    w4 = jnp.concatenate([w_a[0], w_i[0], w_a[1], w_i[1]], axis=-1).astype(BF16)
    b4 = jnp.stack([b_a[0].reshape(nb, C), b_i[0].reshape(nb, C), b_a[1].reshape(nb, C), b_i[1].reshape(nb, C)],
                   axis=1).reshape(nb, 1, 4 * C).astype(F32)
    coef = (-RG_C) * jax.nn.softplus(-lam.astype(F32))
    xb = xg.dtype.itemsize
    vmem = 4 * S * C * xb + 2 * S * C * 2 + 4 * S * C * 4 + 3 * S * 4 * C * 4 + 8 * S * C * 4 + 4 * 2**20
    return pl.pallas_call(
        _rg_kernel,
        grid=(B, nb),
        in_specs=[pl.BlockSpec((S, C), lambda b, n: (b, n)),
                  pl.BlockSpec((S, C), lambda b, n: (b, nb + n)),
                  pl.BlockSpec((4, C), lambda b, n: (0, n)),
                  pl.BlockSpec((1, C), lambda b, n: (0, n)),
                  pl.BlockSpec((None, C, 4 * C), lambda b, n: (n, 0, 0)),
                  pl.BlockSpec((None, 1, 4 * C), lambda b, n: (n, 0, 0)),
                  pl.BlockSpec((2, C), lambda b, n: (0, n))],
        out_specs=pl.BlockSpec((S, C), lambda b, n: (b, n)),
        out_shape=jax.ShapeDtypeStruct((T, R), BF16),
        scratch_shapes=[pltpu.VMEM((2, S, C), F32)] * 3,
        compiler_params=_params(("parallel", "parallel"), vmem),
        name="rglru_core",
    )(xg, xg, conv_w.astype(F32), conv_b.reshape(1, R).astype(F32), w4, b4, coef)


def _first_top2(vals):
    m1 = vals[0]
    j1 = jnp.zeros(vals[0].shape, jnp.int32)
    for j in range(1, len(vals)):
        upd = vals[j] > m1
        m1 = jnp.where(upd, vals[j], m1)
        j1 = jnp.where(upd, j, j1)
    m2 = jnp.full(vals[0].shape, -jnp.inf, F32)
    j2 = jnp.zeros(vals[0].shape, jnp.int32)
    for j in range(len(vals)):
        upd = (j1 != j) & (vals[j] > m2)
        m2 = jnp.where(upd, vals[j], m2)
        j2 = jnp.where(upd, j, j2)
    return m1, j1, m2, j2


def _router_kernel(x_ref, rw_ref, rb_ref, idx_ref, gate_ref):
    G = N_GROUPS
    J = N_EXPERTS // G
    logits = lax.dot_general(rw_ref[...], x_ref[...], (((1,), (1,)), ((), ())), preferred_element_type=F32)
    s = jax.nn.sigmoid(logits)
    biased = s + rb_ref[...]
    a = [biased[j * G:(j + 1) * G, :] for j in range(J)]
    sv = [s[j * G:(j + 1) * G, :] for j in range(J)]
    m1, _, m2, _ = _first_top2(a)
    gscore = m1 + m2
    gidx = lax.broadcasted_iota(jnp.int32, gscore.shape, 0)
    gmax = jnp.max(gscore, axis=0, keepdims=True)
    gsel = jnp.min(jnp.where(gscore == gmax, gidx, G), axis=0, keepdims=True)
    mask = gidx == gsel
    av = [jnp.sum(jnp.where(mask, a[j], 0.0), axis=0, keepdims=True) for j in range(J)]
    sw = [jnp.sum(jnp.where(mask, sv[j], 0.0), axis=0, keepdims=True) for j in range(J)]
    _, j1, _, j2 = _first_top2(av)
    w1 = jnp.zeros_like(sw[0])
    w2 = jnp.zeros_like(sw[0])
    for j in range(J):
        w1 = jnp.where(j1 == j, sw[j], w1)
        w2 = jnp.where(j2 == j, sw[j], w2)
    tot = w1 + w2
    idx_ref[0:1, :] = gsel * J + j1
    idx_ref[1:2, :] = gsel * J + j2
    gate_ref[0:1, :] = w1 / tot
    gate_ref[1:2, :] = w2 / tot


def _router(xb, router_w, router_b):
    T, D = xb.shape
    G = N_GROUPS
    J = N_EXPERTS // G
    perm = jnp.asarray([(r % G) * J + r // G for r in range(N_EXPERTS)], jnp.int32)
    rw = router_w.T[perm].astype(BF16)
    rb = router_b[perm].reshape(N_EXPERTS, 1).astype(F32)
    tm = _tile(T, 1024, LANES)
    vmem = 2 * tm * D * 2 + 2 * N_EXPERTS * D * 2 + 64 * tm * 4 * 4 + 4 * 2**20
    return pl.pallas_call(
        _router_kernel,
        grid=(T // tm,),
        in_specs=[pl.BlockSpec((tm, D), lambda i: (i, 0)),
                  pl.BlockSpec((N_EXPERTS, D), lambda i: (0, 0)),
                  pl.BlockSpec((N_EXPERTS, 1), lambda i: (0, 0))],
        out_specs=[pl.BlockSpec((2, tm), lambda i: (0, i)), pl.BlockSpec((2, tm), lambda i: (0, i))],
        out_shape=[jax.ShapeDtypeStruct((2, T), jnp.int32), jax.ShapeDtypeStruct((2, T), F32)],
        compiler_params=_params(("parallel",), vmem),
        name="router",
    )(xb, rw, rb)


def _slab_copy(xp_hbm, tok, buf, slot, r, sem, ns, pitch):
    return pltpu.make_async_copy(xp_hbm.at[pl.ds(tok * ns, ns)], buf.at[slot, pl.ds(r * pitch, ns)], sem.at[slot])


def _gather_kernel(nused_ref, tok_ref, tok_next_ref, xp_hbm, o_ref, buf, sem, *, tg, ns, pitch):
    i = pl.program_id(0)
    nu = nused_ref[0]
    half = ns * LANES

    def issue(tref, slot):
        def body(r2, c):
            for p in range(2):
                r = 2 * r2 + p
                _slab_copy(xp_hbm, tref[0, r], buf, slot, r, sem, ns, pitch).start(priority=p)
            return c

        lax.fori_loop(0, tg // 2, body, 0, unroll=4)

    def consume(slot):
        def body(r, c):
            _slab_copy(xp_hbm, 0, buf, slot, r, sem, ns, pitch).wait()
            return c

        lax.fori_loop(0, tg, body, 0, unroll=8)
        for c in range(ns):
            w = buf[slot, pl.ds(c, tg, stride=pitch), :]
            lo = pltpu.bitcast(w << 16, F32)
            hi = pltpu.bitcast(w & jnp.uint32(0xFFFF0000), F32)
            o_ref[:, c * LANES:(c + 1) * LANES] = lo.astype(o_ref.dtype)
            o_ref[:, half + c * LANES:half + (c + 1) * LANES] = hi.astype(o_ref.dtype)

    for slot in range(2):
        @pl.when((i < nu) & (i % 2 == slot))
        def _(slot=slot):
            if slot == 0:
                @pl.when(i == 0)
                def _():
                    issue(tok_ref, 0)

            @pl.when(i + 1 < nu)
            def _():
                issue(tok_next_ref, 1 - slot)

            consume(slot)

    @pl.when(i >= nu)
    def _():
        o_ref[...] = jnp.zeros(o_ref.shape, o_ref.dtype)


def _moe_gather(xp, buf_tok, n_used_rows, rows, D):
    ns = D // 2 // LANES
    pitch = ns + SUBLANES
    tg = MOE_TG
    nblk = rows // tg
    tok3 = buf_tok.reshape(nblk, 1, tg)
    grid_spec = pltpu.PrefetchScalarGridSpec(
        num_scalar_prefetch=1,
        grid=(nblk,),
        in_specs=[pl.BlockSpec((None, 1, tg), lambda i, nu: (i, 0, 0), memory_space=pltpu.SMEM),
                  pl.BlockSpec((None, 1, tg), lambda i, nu: (jnp.minimum(i + 1, nblk - 1), 0, 0),
                               memory_space=pltpu.SMEM),
                  pl.BlockSpec(memory_space=pl.ANY)],
        out_specs=pl.BlockSpec((tg, D), lambda i, nu: (i, 0)),
        scratch_shapes=[pltpu.VMEM((2, tg * pitch, LANES), jnp.uint32), pltpu.SemaphoreType.DMA((2,))],
    )
    vmem = 2 * tg * pitch * LANES * 4 + 2 * tg * D * 2 + 4 * tg * D * 4 + 4 * 2**20
    return pl.pallas_call(
        functools.partial(_gather_kernel, tg=tg, ns=ns, pitch=pitch),
        grid_spec=grid_spec,
        out_shape=jax.ShapeDtypeStruct((rows, D), BF16),
        compiler_params=_params(("arbitrary",), vmem),
        name="moe_gather",
    )((n_used_rows // tg).reshape(1).astype(jnp.int32), tok3, tok3, xp)


def _expert_changed(be_ref, b):
    return (b == 0) | (be_ref[b] != be_ref[jnp.maximum(b - 1, 0)])


def _for_valid_rows(nv, bm, fn):
    nsub = (nv + MOE_SUB - 1) // MOE_SUB
    for k in range(1, bm // MOE_SUB + 1):
        @pl.when(nsub == k)
        def _(k=k):
            fn(k * MOE_SUB)


def _expert_up_kernel(be_ref, nv_ref, nu_ref, x_ref, wg_ref, wu_ref, h_ref, wgb, wub):
    del nu_ref
    b = pl.program_id(1)
    nv = nv_ref[b]
    bm = x_ref.shape[0]

    @pl.when((nv > 0) & _expert_changed(be_ref, b))
    def _():
        wgb[...] = wg_ref[...].astype(BF16)
        wub[...] = wu_ref[...].astype(BF16)

    def compute(rows):
        x = x_ref[0:rows, :]
        g = jnp.dot(x, wgb[...], preferred_element_type=F32)
        u = jnp.dot(x, wub[...], preferred_element_type=F32)
        h_ref[0:rows, :] = (jax.nn.silu(g) * u).astype(h_ref.dtype)
        if rows < bm:
            h_ref[rows:, :] = jnp.zeros((bm - rows, h_ref.shape[1]), h_ref.dtype)

    _for_valid_rows(nv, bm, compute)

    @pl.when(nv == 0)
    def _():
        h_ref[...] = jnp.zeros(h_ref.shape, h_ref.dtype)


def _expert_down_kernel(be_ref, nv_ref, nu_ref, h_ref, wd_ref, o_ref, wdb):
    del nu_ref
    b = pl.program_id(1)
    nv = nv_ref[b]
    bm = h_ref.shape[0]

    @pl.when((nv > 0) & _expert_changed(be_ref, b))
    def _():
        wdb[...] = wd_ref[...].astype(BF16)

    def compute(rows):
        acc = jnp.dot(h_ref[0:rows, :], wdb[...], preferred_element_type=F32)
        half = acc.shape[1] // 2
        lo = pltpu.bitcast(acc[:, :half].astype(BF16).astype(F32), jnp.uint32) >> 16
        hi = pltpu.bitcast(acc[:, half:].astype(BF16).astype(F32), jnp.uint32) & jnp.uint32(0xFFFF0000)
        w = hi | lo
        for c in range(half // LANES):
            o_ref[0:rows, c, :] = w[:, c * LANES:(c + 1) * LANES]
        if rows < bm:
            o_ref[rows:, :, :] = jnp.zeros((bm - rows,) + o_ref.shape[1:], o_ref.dtype)

    _for_valid_rows(nv, bm, compute)

    @pl.when(nv == 0)
    def _():
        o_ref[...] = jnp.zeros(o_ref.shape, o_ref.dtype)


def _moe_experts(xs, w_gu, w_dn, layer, blk_e, blk_valid, n_used_blocks):
    rows, D = xs.shape
    F = w_gu.shape[-1] // 2
    bm = MOE_BM
    nb = rows // bm
    tf = _tile(F, MOE_TF, LANES)
    nf = F // tf
    td = _tile(D, max(256, (8 * 2**20) // (F * 4)), 256 if D % 256 == 0 else LANES)
    nd = D // td
    sl = td // 2 // LANES
    nu = n_used_blocks.reshape(1).astype(jnp.int32)

    def blk(b, nu_ref):
        return jnp.minimum(b, nu_ref[0] - 1)

    up_spec = pltpu.PrefetchScalarGridSpec(
        num_scalar_prefetch=3,
        grid=(nf, nb),
        in_specs=[
            pl.BlockSpec((bm, D), lambda f, b, be, nv, nu_: (blk(b, nu_), 0)),
            pl.BlockSpec((None, None, D, tf), lambda f, b, be, nv, nu_: (layer, be[blk(b, nu_)], 0, f)),
            pl.BlockSpec((None, None, D, tf), lambda f, b, be, nv, nu_: (layer, be[blk(b, nu_)], 0, nf + f)),
        ],
        out_specs=pl.BlockSpec((bm, tf), lambda f, b, be, nv, nu_: (b, f)),
        scratch_shapes=[pltpu.VMEM((D, tf), BF16), pltpu.VMEM((D, tf), BF16)],
    )
    vmem = 2 * bm * D * 2 + 4 * D * tf * 4 + 2 * D * tf * 2 + 2 * bm * tf * 2 + 6 * bm * tf * 4 + 4 * 2**20
    hidden = pl.pallas_call(
        _expert_up_kernel,
        grid_spec=up_spec,
        out_shape=jax.ShapeDtypeStruct((rows, F), BF16),
        compiler_params=_params(("arbitrary", "arbitrary"), vmem),
        name="moe_up",
    )(blk_e, blk_valid, nu, xs, w_gu, w_gu)

    down_spec = pltpu.PrefetchScalarGridSpec(
        num_scalar_prefetch=3,
        grid=(nd, nb),
        in_specs=[
            pl.BlockSpec((bm, F), lambda d, b, be, nv, nu_: (blk(b, nu_), 0)),
            pl.BlockSpec((None, None, F, td), lambda d, b, be, nv, nu_: (layer, be[blk(b, nu_)], 0, d)),
        ],
        out_specs=pl.BlockSpec((bm, None, sl, LANES), lambda d, b, be, nv, nu_: (b, d, 0, 0)),
        scratch_shapes=[pltpu.VMEM((F, td), BF16)],
    )
    vmem = 2 * bm * F * 2 + 2 * F * td * 4 + F * td * 2 + 2 * bm * td * 2 + 4 * bm * td * 4 + 4 * 2**20
    ys = pl.pallas_call(
        _expert_down_kernel,
        grid_spec=down_spec,
        out_shape=jax.ShapeDtypeStruct((rows, nd, sl, LANES), jnp.uint32),
        compiler_params=_params(("arbitrary", "arbitrary"), vmem),
        name="moe_down",
    )(blk_e, blk_valid, nu, hidden, w_dn)
    return ys.reshape(rows * nd * sl, LANES), td


def _combine_ln_kernel(pos_ref, pos_next_ref, x_ref, gt_ref, ys_hbm, g_ref, b_ref, o_ref, ob_ref, buf, sem,
                       *, tc, ns, pitch, td, head_blocks):
    i = pl.program_id(0)
    n = pl.num_programs(0)
    sl = td // 2 // LANES

    def copy(k, pos, slot, r):
        return pltpu.make_async_copy(ys_hbm.at[pl.ds(pos * ns, ns)], buf.at[slot, k, pl.ds(r * pitch, ns)],
                                     sem.at[slot])

    def issue(pref, slot):
        def body(r, c):
            for k in range(2):
                copy(k, pref[0, k * tc + r], slot, r).start(priority=k)
            return c

        lax.fori_loop(0, tc, body, 0, unroll=4)

    def consume(slot):
        def body(r, c):
            for k in range(2):
                copy(k, 0, slot, r).wait()
            return c

        lax.fori_loop(0, tc, body, 0, unroll=4)
        g0 = gt_ref[:, 0:1]
        g1 = gt_ref[:, 1:2]
        tiles = [None] * (2 * ns)
        for s in range(ns):
            d, c = divmod(s, sl)
            w0 = buf[slot, 0, pl.ds(s, tc, stride=pitch), :]
            w1 = buf[slot, 1, pl.ds(s, tc, stride=pitch), :]
            lo = pltpu.bitcast(w0 << 16, F32) * g0 + pltpu.bitcast(w1 << 16, F32) * g1
            hi = (pltpu.bitcast(w0 & jnp.uint32(0xFFFF0000), F32) * g0
                  + pltpu.bitcast(w1 & jnp.uint32(0xFFFF0000), F32) * g1)
            tiles[d * 2 * sl + c] = lo
            tiles[d * 2 * sl + sl + c] = hi
        y = DEEPNORM_ALPHA * x_ref[...] + jnp.concatenate(tiles, axis=1)
        out = _layer_norm_rows(y, g_ref[...], b_ref[...])
        if head_blocks is None:
            o_ref[...] = out
            ob_ref[...] = out.astype(BF16)
        else:
            @pl.when(i < head_blocks)
            def _():
                o_ref[...] = out

            @pl.when(i >= head_blocks)
            def _():
                ob_ref[...] = out

    for slot in range(2):
        @pl.when(i % 2 == slot)
        def _(slot=slot):
            if slot == 0:
                @pl.when(i == 0)
                def _():
                    issue(pos_ref, 0)

            @pl.when(i + 1 < n)
            def _():
                issue(pos_next_ref, 1 - slot)

            consume(slot)


def _moe_combine_ln(x, ys, td, dest, gate, g, b, head_rows=None):
    T, D = x.shape
    ns = D // 2 // LANES
    pitch = ns + SUBLANES
    tc = _tile(T, 256, SUBLANES) if head_rows is None else _tile(math.gcd(T, head_rows), 256, SUBLANES)
    nblk = T // tc
    pos = dest.reshape(2, nblk, tc).transpose(1, 0, 2).reshape(nblk, 1, 2 * tc)
    row = pl.BlockSpec((tc, D), lambda i: (i, 0))
    vec = pl.BlockSpec((1, D), lambda i: (0, 0))
    if head_rows is None:
        head_blocks = None
        out_specs = [row, row]
        out_shape = [jax.ShapeDtypeStruct((T, D), F32), jax.ShapeDtypeStruct((T, D), BF16)]
    else:
        head_blocks = head_rows // tc
        out_specs = [pl.BlockSpec((tc, D), lambda i: (jnp.minimum(i, head_blocks - 1), 0)),
                     pl.BlockSpec((tc, D), lambda i: (jnp.maximum(i - head_blocks, 0), 0))]
        out_shape = [jax.ShapeDtypeStruct((head_rows, D), F32), jax.ShapeDtypeStruct((T - head_rows, D), F32)]
    vmem = 4 * tc * pitch * LANES * 4 + 2 * tc * D * 4 + 2 * tc * D * (4 + 4) + 6 * tc * D * 4 + 4 * 2**20
    return pl.pallas_call(
        functools.partial(_combine_ln_kernel, tc=tc, ns=ns, pitch=pitch, td=td, head_blocks=head_blocks),
        grid=(nblk,),
        in_specs=[pl.BlockSpec((None, 1, 2 * tc), lambda i: (i, 0, 0), memory_space=pltpu.SMEM),
                  pl.BlockSpec((None, 1, 2 * tc), lambda i: (jnp.minimum(i + 1, nblk - 1), 0, 0),
                               memory_space=pltpu.SMEM),
                  row, pl.BlockSpec((tc, 2), lambda i: (i, 0)), pl.BlockSpec(memory_space=pl.ANY), vec, vec],
        out_specs=out_specs,
        out_shape=out_shape,
        scratch_shapes=[pltpu.VMEM((2, 2, tc * pitch, LANES), jnp.uint32), pltpu.SemaphoreType.DMA((2,))],
        compiler_params=_params(("arbitrary",), vmem),
        name="moe_combine_ln",
    )(pos, pos, x, gate.T, ys, g.reshape(1, D), b.reshape(1, D))


def _moe_layout(idx, T):
    E = N_EXPERTS
    bm = MOE_BM
    nb = -(-2 * T // bm) + E
    rows = nb * bm
    flat_e = idx.reshape(-1)
    tok = jnp.tile(jnp.arange(T, dtype=jnp.int32), 2)
    onehot = (flat_e[:, None] == jnp.arange(E, dtype=jnp.int32)[None, :]).astype(jnp.int32)
    csum = jnp.cumsum(onehot, axis=0)
    rank = jnp.sum(onehot * csum, axis=1) - 1
    counts = csum[-1]
    padded = (counts + bm - 1) // bm * bm
    pad_end = jnp.cumsum(padded)
    pad_start = pad_end - padded
    dest = (pad_start[flat_e] + rank).astype(jnp.int32)
    buf_tok = jnp.zeros((rows,), jnp.int32).at[dest].set(tok)
    blk_start = jnp.arange(nb, dtype=jnp.int32) * bm
    blk_e = jnp.minimum(jnp.searchsorted(pad_end, blk_start, side="right"), E - 1).astype(jnp.int32)
    blk_valid = jnp.clip(pad_start[blk_e] + counts[blk_e] - blk_start, 0, bm).astype(jnp.int32)
    n_used_rows = pad_end[-1].astype(jnp.int32)
    return dest, buf_tok, blk_e, blk_valid, n_used_rows, rows


def _routed_experts_ln(x, xb, xp, router_w, router_b, w_gu, w_dn, layer, g, b, head_rows=None):
    T, D = x.shape
    idx, gate = _router(xb, router_w, router_b)
    dest, buf_tok, blk_e, blk_valid, n_used_rows, rows = _moe_layout(idx, T)
    xs = _moe_gather(xp, buf_tok, n_used_rows, rows, D)
    ys, td = _moe_experts(xs, w_gu, w_dn, layer, blk_e, blk_valid, n_used_rows // MOE_BM)
    return _moe_combine_ln(x, ys, td, dest, gate, g, b, head_rows)


def kernel(x_prompt, x_sample, mem_prompt, mem_sample, da_w_qkv, da_lam, da_subln_g, da_w_o, sg_w_in, sg_b_in, sg_ln_g, sg_ln_b, sg_w_s, sg_b_s, sg_w_o, rg_w_in, rg_conv_w, rg_conv_b, rg_w_a, rg_b_a, rg_w_i, rg_b_i, rg_lambda, rg_w_o, ca_w_q, ca_w_kv, ca_w_o, router_w, router_b, moe_w_gu, moe_w_dn, ln_g, ln_b):
    Bp, S, D = x_prompt.shape
    Bs = x_sample.shape[0]
    assert x_sample.shape[1] == S, "both request groups must share the sequence length"
    B = Bp + Bs
    M = mem_prompt.shape[1]
    x = jnp.concatenate([x_prompt.reshape(Bp * S, D), x_sample.reshape(Bs * S, D)], axis=0)
    xb = x.astype(BF16)
    mem = jnp.concatenate([mem_prompt.reshape(Bp * M, D), mem_sample.reshape(Bs * M, D)], axis=0).astype(BF16)
    slopes = jnp.asarray([2.0 ** (-8.0 * (h + 1) / DA_HEADS) for h in range(DA_HEADS)], F32)
    dh = D // (2 * DA_HEADS)
    qkv_scale = jnp.concatenate([jnp.full((D,), LOG2E * dh ** -0.5, F32), jnp.ones((2 * D,), F32)])

    for i in range(DEPTH):
        kind, slot = i % N_MIXERS, i // N_MIXERS
        if kind == 0:
            lam_init = 0.8 - 0.6 * math.exp(-0.3 * i)
            lf = da_lam[slot].astype(F32)
            lam_full = (jnp.exp(jnp.sum(lf[0] * lf[1])) - jnp.exp(jnp.sum(lf[2] * lf[3])) + lam_init).reshape(1)
            qkv = _mm(xb, da_w_qkv, slot, col_scale=qkv_scale, out_dtype=BF16, name="da_qkv")
            o = _diff_attention(qkv, B, S, slopes, lam_full, da_subln_g[slot], lam_init)
            h = _mm(o, da_w_o, slot, name="da_out")
        elif kind == 1:
            z = _mm(xb, sg_w_in.astype(BF16), slot, bias=sg_b_in[slot], act="gelu", out_dtype=BF16, name="sg_in")
            gated = _spatial_gate(z, sg_ln_g[slot], sg_ln_b[slot], sg_w_s[slot], sg_b_s[slot])
            h = _mm(gated, sg_w_o.astype(BF16), slot, name="sg_out")
        else:
            xg = _mm(xb, rg_w_in, slot, name="rg_in")
            hr = _rglru_core(xg, B, S, rg_conv_w[slot], rg_conv_b[slot], rg_w_a[slot], rg_b_a[slot],
                             rg_w_i[slot], rg_b_i[slot], rg_lambda[slot])
            h = _mm(hr, rg_w_o, slot, name="rg_out")
        x, xb = _add_ln(x, h, ln_g[i, 0], ln_b[i, 0])

        q = _mm(xb, ca_w_q, i, out_dtype=BF16, name="ca_q")
        kv = _mm(mem, ca_w_kv, i, out_dtype=BF16, name="ca_kv")
        o = _memory_attention_core(q, kv, B, S, M)
        h = _mm(o, ca_w_o, i, name="ca_out")
        x, xb, xp = _add_ln(x, h, ln_g[i, 1], ln_b[i, 1], emit_packed=True)

        x, xb = _routed_experts_ln(x, xb, xp, router_w, router_b, moe_w_gu, moe_w_dn, i, ln_g[i, 2], ln_b[i, 2],
                                   head_rows=Bp * S if i == DEPTH - 1 else None)

    return (x.reshape(Bp, S, D), xb.reshape(Bs, S, D))
```

```python
import functools
import math

import jax
import jax.numpy as jnp
from jax import lax
from jax.experimental import pallas as pl
from jax.experimental.pallas import tpu as pltpu

DEPTH = 4
N_MIXERS = 3
DA_HEADS = 16
SG_CHUNK = 128
SG_GROUPS = 16
RG_BLOCKS = 16
RG_C = 8.0
CA_HEADS = 4
N_EXPERTS = 32
N_GROUPS = 8
LN_EPS = 1e-5
DEEPNORM_ALPHA = (2 * DEPTH) ** 0.25
LOG2E = math.log2(math.e)

V7X_VMEM_BYTES = 64 * 1024 * 1024
VMEM_CAP = V7X_VMEM_BYTES - 6 * 1024 * 1024
SUBLANES = 8
LANES = 128

MOE_BM = 512
MOE_SUB = 128
MOE_TF = 512
MOE_TG = 512

BF16 = jnp.bfloat16
F32 = jnp.float32


def _tile(dim, pref, align):
    if dim <= pref:
        return dim
    t = (pref // align) * align
    while t > align and dim % t:
        t -= align
    assert dim % t == 0, (dim, pref, align)
    return t


def _params(sem, vmem_bytes):
    return pltpu.CompilerParams(dimension_semantics=sem,
                                vmem_limit_bytes=int(min(max(vmem_bytes, 16 * 2**20), VMEM_CAP)))


def _mm_kernel(*refs, act, has_bias, has_scale, cast_w):
    x_ref, w_ref = refs[0], refs[1]
    if cast_w:
        o_ref, wbf_ref = refs[-2], refs[-1]
        extra = list(refs[2:-2])

        @pl.when(pl.program_id(1) == 0)
        def _():
            wbf_ref[...] = w_ref[...].astype(BF16)
    else:
        o_ref, wbf_ref = refs[-1], w_ref
        extra = list(refs[2:-1])

    acc = jnp.dot(x_ref[...].astype(BF16), wbf_ref[...], preferred_element_type=F32)
    if has_bias:
        acc = acc + extra.pop(0)[...]
    if has_scale:
        acc = acc * extra.pop(0)[...]
    if act == "gelu":
        acc = jax.nn.gelu(acc)
    o_ref[...] = acc.astype(o_ref.dtype)


def _mm(x, w, layer, bias=None, col_scale=None, act=None, out_dtype=F32, name="mm"):
    M, K = x.shape
    _, K2, N = w.shape
    assert K == K2
    xb = x.dtype.itemsize
    wb = w.dtype.itemsize
    cast_w = w.dtype != BF16
    ob = jnp.dtype(out_dtype).itemsize
    x_budget = (12 if cast_w else 8) * 2**20
    tm = _tile(M, max(256, x_budget // (K * xb)), 256 if M % 256 == 0 else SUBLANES)
    tn = _tile(N, max(256, (8 * 2**20) // (K * wb)), 256 if N % 256 == 0 else LANES)
    grid = (N // tn, M // tm)
    in_specs = [pl.BlockSpec((tm, K), lambda j, i: (i, 0)),
                pl.BlockSpec((None, K, tn), lambda j, i: (layer, 0, j))]
    args = [x, w]
    for vec in (bias, col_scale):
        if vec is not None:
            in_specs.append(pl.BlockSpec((1, tn), lambda j, i: (0, j)))
            args.append(vec.reshape(1, N).astype(F32))
    vmem = (2 * tm * K * xb + 2 * K * tn * wb + (K * tn * 2 if cast_w else 0) + 2 * tm * tn * ob
            + 2 * tm * tn * 4 + 4 * 2**20)
    return pl.pallas_call(
        functools.partial(_mm_kernel, act=act, has_bias=bias is not None, has_scale=col_scale is not None,
                          cast_w=cast_w),
        grid=grid,
        in_specs=in_specs,
        out_specs=pl.BlockSpec((tm, tn), lambda j, i: (i, j)),
        out_shape=jax.ShapeDtypeStruct((M, N), out_dtype),
        scratch_shapes=[pltpu.VMEM((K, tn), BF16)] if cast_w else [],
        compiler_params=_params(("parallel", "arbitrary"), vmem),
        name=name,
    )(*args)


def _layer_norm_rows(y, g, b):
    mu = jnp.mean(y, axis=-1, keepdims=True)
    yc = y - mu
    var = jnp.mean(yc * yc, axis=-1, keepdims=True)
    return yc * lax.rsqrt(var + LN_EPS) * g + b


def _pack_slabs(out, xp_ref):
    tm, D = out.shape
    half = D // 2
    ns = half // LANES
    lo = pltpu.bitcast(out[:, :half].astype(BF16).astype(F32), jnp.uint32) >> 16
    hi = pltpu.bitcast(out[:, half:].astype(BF16).astype(F32), jnp.uint32) & jnp.uint32(0xFFFF0000)
    w = hi | lo
    for c in range(ns):
        xp_ref[pl.ds(c, tm, stride=ns), :] = w[:, c * LANES:(c + 1) * LANES]


def _add_ln_kernel(x_ref, h_ref, g_ref, b_ref, o_ref, ob_ref, *maybe_xp_ref):
    y = DEEPNORM_ALPHA * x_ref[...] + h_ref[...].astype(F32)
    out = _layer_norm_rows(y, g_ref[...], b_ref[...])
    o_ref[...] = out
    ob_ref[...] = out.astype(BF16)
    if maybe_xp_ref:
        _pack_slabs(out, maybe_xp_ref[0])


def _add_ln(x, h, g, b, emit_packed=False):
    T, D = x.shape
    tm = _tile(T, 256, SUBLANES)
    ns = D // 2 // LANES
    row = pl.BlockSpec((tm, D), lambda i: (i, 0))
    vec = pl.BlockSpec((1, D), lambda i: (0, 0))
    out_specs = [row, row]
    out_shape = [jax.ShapeDtypeStruct((T, D), F32), jax.ShapeDtypeStruct((T, D), BF16)]
    if emit_packed:
        out_specs.append(pl.BlockSpec((tm * ns, LANES), lambda i: (i, 0)))
        out_shape.append(jax.ShapeDtypeStruct((T * ns, LANES), jnp.uint32))
    vmem = 2 * tm * D * (4 + h.dtype.itemsize + 4 + 2 + 2) + 6 * tm * D * 4
    return pl.pallas_call(
        _add_ln_kernel,
        grid=(T // tm,),
        in_specs=[row, row, vec, vec],
        out_specs=out_specs,
        out_shape=out_shape,
        compiler_params=_params(("parallel",), vmem),
        name="add_ln",
    )(x, h, g.reshape(1, D), b.reshape(1, D))


def _da_kernel(slope_ref, lam_ref, q_ref, k_ref, v_ref, g_ref, o_ref, bias_ref, *, tq, sub, lam_init):
    h = pl.program_id(0)
    qi = pl.program_id(1)
    S = k_ref.shape[0]
    dh = q_ref.shape[1] // 2

    @pl.when(pl.program_id(2) == 0)
    def _():
        qpos = qi * tq + lax.broadcasted_iota(jnp.int32, (tq, S), 0)
        kpos = lax.broadcasted_iota(jnp.int32, (tq, S), 1)
        bias_ref[...] = jnp.abs(qpos - kpos).astype(F32) * (-LOG2E * slope_ref[h])

    lam = lam_ref[0]
    k = k_ref[...]
    v = v_ref[...]

    def unnormalised(q_rows, rows, c):
        s = lax.dot_general(q_rows[:, c * dh:(c + 1) * dh], k[:, c * dh:(c + 1) * dh],
                            (((1,), (1,)), ((), ())), preferred_element_type=F32)
        xs = [s[:, j * LANES:(j + 1) * LANES] + bias_ref[rows, j * LANES:(j + 1) * LANES]
              for j in range(S // LANES)]
        m = functools.reduce(jnp.maximum, xs)
        m = jnp.max(m, axis=-1, keepdims=True)
        es = [jnp.exp2(x - m) for x in xs]
        l = jnp.sum(functools.reduce(lambda p, t: p + t, es), axis=-1, keepdims=True)
        e = jnp.concatenate([t.astype(BF16) for t in es], axis=1)
        return jnp.dot(e, v, preferred_element_type=F32), l

    for r0 in range(0, tq, sub):
        rows = slice(r0, r0 + sub)
        q_rows = q_ref[rows, :]
        pv1, l1 = unnormalised(q_rows, rows, 0)
        pv2, l2 = unnormalised(q_rows, rows, 1)
        o = pv1 * (1.0 / l1) - pv2 * (lam / l2)
        ms = jnp.mean(o * o, axis=-1, keepdims=True)
        o = o * lax.rsqrt(ms + LN_EPS) * g_ref[...] * (1.0 - lam_init)
        o_ref[rows, :] = o.astype(o_ref.dtype)


def _diff_attention(qkv, B, S, slopes, lam_full, subln_g, lam_init):
    T, D3 = qkv.shape
    D = D3 // 3
    H = DA_HEADS
    hw = D // H
    tq = _tile(S, 512, SUBLANES)
    nq = S // tq
    smem = pl.BlockSpec(memory_space=pltpu.SMEM)
    vmem = 10 * tq * S * 4 + 8 * S * hw * 2 + 4 * 2**20
    return pl.pallas_call(
        functools.partial(_da_kernel, tq=tq, sub=_tile(tq, 128, SUBLANES), lam_init=lam_init),
        grid=(H, nq, B),
        in_specs=[smem, smem,
                  pl.BlockSpec((tq, hw), lambda h, qi, b: (b * nq + qi, h)),
                  pl.BlockSpec((S, hw), lambda h, qi, b: (b, H + h)),
                  pl.BlockSpec((S, hw), lambda h, qi, b: (b, 2 * H + h)),
                  pl.BlockSpec((1, hw), lambda h, qi, b: (0, 0))],
        out_specs=pl.BlockSpec((tq, hw), lambda h, qi, b: (b * nq + qi, h)),
        out_shape=jax.ShapeDtypeStruct((T, D), BF16),
        scratch_shapes=[pltpu.VMEM((tq, S), F32)],
        compiler_params=_params(("parallel", "parallel", "arbitrary"), vmem),
        name="diff_attn",
    )(slopes, lam_full, qkv, qkv, qkv, subln_g.reshape(1, hw).astype(F32))


def _ca_kernel(q_ref, k_ref, v_ref, o_ref):
    dh = q_ref.shape[1]
    s = lax.dot_general(q_ref[...], k_ref[...], (((1,), (1,)), ((), ())),
                        preferred_element_type=F32) * (dh ** -0.5)
    m = jnp.max(s, axis=-1, keepdims=True)
    e = jnp.exp(s - m)
    p = e * (1.0 / jnp.sum(e, axis=-1, keepdims=True))
    o_ref[...] = jnp.dot(p.astype(BF16), v_ref[...], preferred_element_type=F32).astype(o_ref.dtype)


def _memory_attention_core(q, kv, B, S, M):
    T, D = q.shape
    H = CA_HEADS
    dh = D // H
    tq = _tile(S, 2048, SUBLANES)
    nq = S // tq
    vmem = 4 * tq * dh * 2 * 2 + 4 * M * dh * 2 + 4 * tq * M * 4 + 2 * tq * dh * 4 + 4 * 2**20
    return pl.pallas_call(
        _ca_kernel,
        grid=(B, H, nq),
        in_specs=[pl.BlockSpec((tq, dh), lambda b, h, qi: (b * nq + qi, h)),
                  pl.BlockSpec((M, dh), lambda b, h, qi: (b, h)),
                  pl.BlockSpec((M, dh), lambda b, h, qi: (b, H + h))],
        out_specs=pl.BlockSpec((tq, dh), lambda b, h, qi: (b * nq + qi, h)),
        out_shape=jax.ShapeDtypeStruct((T, D), BF16),
        compiler_params=_params(("parallel", "parallel", "parallel"), vmem),
        name="mem_attn",
    )(q, kv, kv)


def _sg_kernel(u_ref, v_ref, g_ref, b_ref, ws_ref, bs_ref, o_ref, *, n_chunks):
    W = v_ref.shape[1]
    G = ws_ref.shape[0]
    gw = W // G
    vn = _layer_norm_rows(v_ref[...].astype(F32), g_ref[...], b_ref[...]).astype(BF16)
    for c in range(n_chunks):
        rows = slice(c * SG_CHUNK, (c + 1) * SG_CHUNK)
        for g in range(G):
            cols = slice(g * gw, (g + 1) * gw)
            y = jnp.dot(ws_ref[g], vn[rows, cols], preferred_element_type=F32) + bs_ref[:, g:g + 1]
            o_ref[rows, cols] = (u_ref[rows, cols].astype(F32) * y).astype(o_ref.dtype)


def _spatial_gate(z, ln_g, ln_b, w_s, b_s):
    T, W2 = z.shape
    W = W2 // 2
    n_chunks = 2 if T % (2 * SG_CHUNK) == 0 else 1
    R = n_chunks * SG_CHUNK
    G = w_s.shape[0]
    zb = z.dtype.itemsize
    vmem = 4 * R * W * zb + 2 * R * W * 2 + 4 * R * W * 4 + 4 * 2**20
    return pl.pallas_call(
        functools.partial(_sg_kernel, n_chunks=n_chunks),
        grid=(T // R,),
        in_specs=[pl.BlockSpec((R, W), lambda i: (i, 0)),
                  pl.BlockSpec((R, W), lambda i: (i, 1)),
                  pl.BlockSpec((1, W), lambda i: (0, 0)),
                  pl.BlockSpec((1, W), lambda i: (0, 0)),
                  pl.BlockSpec((G, SG_CHUNK, SG_CHUNK), lambda i: (0, 0, 0)),
                  pl.BlockSpec((SG_CHUNK, G), lambda i: (0, 0))],
        out_specs=pl.BlockSpec((R, W), lambda i: (i, 0)),
        out_shape=jax.ShapeDtypeStruct((T, W), BF16),
        compiler_params=_params(("parallel",), vmem),
        name="spatial_gate",
    )(z, z, ln_g.reshape(1, W), ln_b.reshape(1, W), w_s.astype(BF16), b_s.T.astype(F32))


def _rg_kernel(xr_ref, g_ref, cw_ref, cb_ref, w4_ref, b4_ref, c_ref, o_ref, a_s, b_s, h_s):
    S, C = xr_ref.shape
    xr = xr_ref[...].astype(F32)
    t = lax.broadcasted_iota(jnp.int32, (S, C), 0)
    x_m2 = jnp.where(t >= 2, pltpu.roll(xr, 2, 0), 0.0)
    x_m1 = jnp.where(t >= 1, pltpu.roll(xr, 1, 0), 0.0)
    x_p1 = jnp.where(t < S - 1, pltpu.roll(xr, S - 1, 0), 0.0)
    xc = x_m2 * cw_ref[0:1, :] + x_m1 * cw_ref[1:2, :] + xr * cw_ref[2:3, :] + x_p1 * cw_ref[3:4, :] + cb_ref[...]
    pre = jnp.dot(xc.astype(BF16), w4_ref[...], preferred_element_type=F32) + b4_ref[...]
    def sigmoid(z):
        return 0.5 * jnp.tanh(0.5 * z) + 0.5

    for d in range(2):
        r = sigmoid(pre[:, (2 * d) * C:(2 * d + 1) * C])
        i = sigmoid(pre[:, (2 * d + 1) * C:(2 * d + 2) * C])
        a = jnp.exp(c_ref[d:d + 1, :] * r)
        a_s[d] = a
        y = jnp.maximum(1.0 - a * a, 0.0)
        root = jnp.where(y > 0.0, y * lax.rsqrt(y), 0.0)
        b_s[d] = root * (i * xc)

    row = lax.broadcasted_iota(jnp.int32, (SUBLANES, C), 0)
    n_groups = S // SUBLANES

    def group_scan(A, Bv, reverse):
        for sh in (1, 2, 4):
            if reverse:
                ok = row < SUBLANES - sh
                rs = SUBLANES - sh
            else:
                ok = row >= sh
                rs = sh
            A_sh = jnp.where(ok, pltpu.roll(A, rs, 0), 1.0)
            B_sh = jnp.where(ok, pltpu.roll(Bv, rs, 0), 0.0)
            Bv = A * B_sh + Bv
            A = A * A_sh
        return A, Bv

    def body(gi, carry):
        hf, hb = carry
        r0 = pl.multiple_of(gi * SUBLANES, SUBLANES)
        A, Bv = group_scan(a_s[0, pl.ds(r0, SUBLANES), :], b_s[0, pl.ds(r0, SUBLANES), :], False)
        Hf = A * hf + Bv
        h_s[0, pl.ds(r0, SUBLANES), :] = Hf
        r1 = pl.multiple_of((n_groups - 1 - gi) * SUBLANES, SUBLANES)
        A, Bv = group_scan(a_s[1, pl.ds(r1, SUBLANES), :], b_s[1, pl.ds(r1, SUBLANES), :], True)
        Hb = A * hb + Bv
        h_s[1, pl.ds(r1, SUBLANES), :] = Hb
        return Hf[SUBLANES - 1:SUBLANES, :], Hb[0:1, :]

    zero = jnp.zeros((1, C), F32)
    lax.fori_loop(0, n_groups, body, (zero, zero), unroll=8)
    gate = jax.nn.gelu(g_ref[...].astype(F32))
    o_ref[...] = ((h_s[0] + h_s[1]) * gate).astype(o_ref.dtype)


def _rglru_core(xg, B, S, conv_w, conv_b, w_a, b_a, w_i, b_i, lam):
    T, R2 = xg.shape
    R = R2 // 2
    nb = RG_BLOCKS
    C = R // nb
    w4 = jnp.concatenate([w_a[0], w_i[0], w_a[1], w_i[1]], axis=-1).astype(BF16)
    b4 = jnp.stack([b_a[0].reshape(nb, C), b_i[0].reshape(nb, C), b_a[1].reshape(nb, C), b_i[1].reshape(nb, C)],
                   axis=1).reshape(nb, 1, 4 * C).astype(F32)
    coef = (-RG_C) * jax.nn.softplus(-lam.astype(F32))
    xb = xg.dtype.itemsize
    vmem = 4 * S * C * xb + 2 * S * C * 2 + 4 * S * C * 4 + 3 * S * 4 * C * 4 + 8 * S * C * 4 + 4 * 2**20
    return pl.pallas_call(
        _rg_kernel,
        grid=(B, nb),
        in_specs=[pl.BlockSpec((S, C), lambda b, n: (b, n)),
                  pl.BlockSpec((S, C), lambda b, n: (b, nb + n)),
                  pl.BlockSpec((4, C), lambda b, n: (0, n)),
                  pl.BlockSpec((1, C), lambda b, n: (0, n)),
                  pl.BlockSpec((None, C, 4 * C), lambda b, n: (n, 0, 0)),
                  pl.BlockSpec((None, 1, 4 * C), lambda b, n: (n, 0, 0)),
                  pl.BlockSpec((2, C), lambda b, n: (0, n))],
        out_specs=pl.BlockSpec((S, C), lambda b, n: (b, n)),
        out_shape=jax.ShapeDtypeStruct((T, R), BF16),
        scratch_shapes=[pltpu.VMEM((2, S, C), F32)] * 3,
        compiler_params=_params(("parallel", "parallel"), vmem),
        name="rglru_core",
    )(xg, xg, conv_w.astype(F32), conv_b.reshape(1, R).astype(F32), w4, b4, coef)


def _first_top2(vals):
    m1 = vals[0]
    j1 = jnp.zeros(vals[0].shape, jnp.int32)
    for j in range(1, len(vals)):
        upd = vals[j] > m1
        m1 = jnp.where(upd, vals[j], m1)
        j1 = jnp.where(upd, j, j1)
    m2 = jnp.full(vals[0].shape, -jnp.inf, F32)
    j2 = jnp.zeros(vals[0].shape, jnp.int32)
    for j in range(len(vals)):
        upd = (j1 != j) & (vals[j] > m2)
        m2 = jnp.where(upd, vals[j], m2)
        j2 = jnp.where(upd, j, j2)
    return m1, j1, m2, j2


def _router_kernel(x_ref, rw_ref, rb_ref, idx_ref, gate_ref):
    G = N_GROUPS
    J = N_EXPERTS // G
    logits = lax.dot_general(rw_ref[...], x_ref[...], (((1,), (1,)), ((), ())), preferred_element_type=F32)
    s = jax.nn.sigmoid(logits)
    biased = s + rb_ref[...]
    a = [biased[j * G:(j + 1) * G, :] for j in range(J)]
    sv = [s[j * G:(j + 1) * G, :] for j in range(J)]
    m1, _, m2, _ = _first_top2(a)
    gscore = m1 + m2
    gidx = lax.broadcasted_iota(jnp.int32, gscore.shape, 0)
    gmax = jnp.max(gscore, axis=0, keepdims=True)
    gsel = jnp.min(jnp.where(gscore == gmax, gidx, G), axis=0, keepdims=True)
    mask = gidx == gsel
    av = [jnp.sum(jnp.where(mask, a[j], 0.0), axis=0, keepdims=True) for j in range(J)]
    sw = [jnp.sum(jnp.where(mask, sv[j], 0.0), axis=0, keepdims=True) for j in range(J)]
    _, j1, _, j2 = _first_top2(av)
    w1 = jnp.zeros_like(sw[0])
    w2 = jnp.zeros_like(sw[0])
    for j in range(J):
        w1 = jnp.where(j1 == j, sw[j], w1)
        w2 = jnp.where(j2 == j, sw[j], w2)
    tot = w1 + w2
    idx_ref[0:1, :] = gsel * J + j1
    idx_ref[1:2, :] = gsel * J + j2
    gate_ref[0:1, :] = w1 / tot
    gate_ref[1:2, :] = w2 / tot


def _router(xb, router_w, router_b):
    T, D = xb.shape
    G = N_GROUPS
    J = N_EXPERTS // G
    perm = jnp.asarray([(r % G) * J + r // G for r in range(N_EXPERTS)], jnp.int32)
    rw = router_w.T[perm].astype(BF16)
    rb = router_b[perm].reshape(N_EXPERTS, 1).astype(F32)
    tm = _tile(T, 1024, LANES)
    vmem = 2 * tm * D * 2 + 2 * N_EXPERTS * D * 2 + 64 * tm * 4 * 4 + 4 * 2**20
    return pl.pallas_call(
        _router_kernel,
        grid=(T // tm,),
        in_specs=[pl.BlockSpec((tm, D), lambda i: (i, 0)),
                  pl.BlockSpec((N_EXPERTS, D), lambda i: (0, 0)),
                  pl.BlockSpec((N_EXPERTS, 1), lambda i: (0, 0))],
        out_specs=[pl.BlockSpec((2, tm), lambda i: (0, i)), pl.BlockSpec((2, tm), lambda i: (0, i))],
        out_shape=[jax.ShapeDtypeStruct((2, T), jnp.int32), jax.ShapeDtypeStruct((2, T), F32)],
        compiler_params=_params(("parallel",), vmem),
        name="router",
    )(xb, rw, rb)


def _slab_copy(xp_hbm, tok, buf, slot, r, sem, ns, pitch):
    return pltpu.make_async_copy(xp_hbm.at[pl.ds(tok * ns, ns)], buf.at[slot, pl.ds(r * pitch, ns)], sem.at[slot])


def _gather_kernel(nused_ref, tok_ref, tok_next_ref, xp_hbm, o_ref, buf, sem, *, tg, ns, pitch):
    i = pl.program_id(0)
    nu = nused_ref[0]
    half = ns * LANES

    def issue(tref, slot):
        def body(r2, c):
            for p in range(2):
                r = 2 * r2 + p
                _slab_copy(xp_hbm, tref[0, r], buf, slot, r, sem, ns, pitch).start(priority=p)
            return c

        lax.fori_loop(0, tg // 2, body, 0, unroll=4)

    def consume(slot):
        def body(r, c):
            _slab_copy(xp_hbm, 0, buf, slot, r, sem, ns, pitch).wait()
            return c

        lax.fori_loop(0, tg, body, 0, unroll=8)
        for c in range(ns):
            w = buf[slot, pl.ds(c, tg, stride=pitch), :]
            lo = pltpu.bitcast(w << 16, F32)
            hi = pltpu.bitcast(w & jnp.uint32(0xFFFF0000), F32)
            o_ref[:, c * LANES:(c + 1) * LANES] = lo.astype(o_ref.dtype)
            o_ref[:, half + c * LANES:half + (c + 1) * LANES] = hi.astype(o_ref.dtype)

    for slot in range(2):
        @pl.when((i < nu) & (i % 2 == slot))
        def _(slot=slot):
            if slot == 0:
                @pl.when(i == 0)
                def _():
                    issue(tok_ref, 0)

            @pl.when(i + 1 < nu)
            def _():
                issue(tok_next_ref, 1 - slot)

            consume(slot)

    @pl.when(i >= nu)
    def _():
        o_ref[...] = jnp.zeros(o_ref.shape, o_ref.dtype)


def _moe_gather(xp, buf_tok, n_used_rows, rows, D):
    ns = D // 2 // LANES
    pitch = ns + SUBLANES
    tg = MOE_TG
    nblk = rows // tg
    tok3 = buf_tok.reshape(nblk, 1, tg)
    grid_spec = pltpu.PrefetchScalarGridSpec(
        num_scalar_prefetch=1,
        grid=(nblk,),
        in_specs=[pl.BlockSpec((None, 1, tg), lambda i, nu: (i, 0, 0), memory_space=pltpu.SMEM),
                  pl.BlockSpec((None, 1, tg), lambda i, nu: (jnp.minimum(i + 1, nblk - 1), 0, 0),
                               memory_space=pltpu.SMEM),
                  pl.BlockSpec(memory_space=pl.ANY)],
        out_specs=pl.BlockSpec((tg, D), lambda i, nu: (i, 0)),
        scratch_shapes=[pltpu.VMEM((2, tg * pitch, LANES), jnp.uint32), pltpu.SemaphoreType.DMA((2,))],
    )
    vmem = 2 * tg * pitch * LANES * 4 + 2 * tg * D * 2 + 4 * tg * D * 4 + 4 * 2**20
    return pl.pallas_call(
        functools.partial(_gather_kernel, tg=tg, ns=ns, pitch=pitch),
        grid_spec=grid_spec,
        out_shape=jax.ShapeDtypeStruct((rows, D), BF16),
        compiler_params=_params(("arbitrary",), vmem),
        name="moe_gather",
    )((n_used_rows // tg).reshape(1).astype(jnp.int32), tok3, tok3, xp)


def _expert_changed(be_ref, b):
    return (b == 0) | (be_ref[b] != be_ref[jnp.maximum(b - 1, 0)])


def _for_valid_rows(nv, bm, fn):
    nsub = (nv + MOE_SUB - 1) // MOE_SUB
    for k in range(1, bm // MOE_SUB + 1):
        @pl.when(nsub == k)
        def _(k=k):
            fn(k * MOE_SUB)


def _expert_up_kernel(be_ref, nv_ref, nu_ref, x_ref, wg_ref, wu_ref, h_ref, wgb, wub):
    del nu_ref
    b = pl.program_id(1)
    nv = nv_ref[b]
    bm = x_ref.shape[0]

    @pl.when((nv > 0) & _expert_changed(be_ref, b))
    def _():
        wgb[...] = wg_ref[...].astype(BF16)
        wub[...] = wu_ref[...].astype(BF16)

    def compute(rows):
        x = x_ref[0:rows, :]
        g = jnp.dot(x, wgb[...], preferred_element_type=F32)
        u = jnp.dot(x, wub[...], preferred_element_type=F32)
        h_ref[0:rows, :] = (jax.nn.silu(g) * u).astype(h_ref.dtype)
        if rows < bm:
            h_ref[rows:, :] = jnp.zeros((bm - rows, h_ref.shape[1]), h_ref.dtype)

    _for_valid_rows(nv, bm, compute)

    @pl.when(nv == 0)
    def _():
        h_ref[...] = jnp.zeros(h_ref.shape, h_ref.dtype)


def _expert_down_kernel(be_ref, nv_ref, nu_ref, h_ref, wd_ref, o_ref, wdb):
    del nu_ref
    b = pl.program_id(1)
    nv = nv_ref[b]
    bm = h_ref.shape[0]

    @pl.when((nv > 0) & _expert_changed(be_ref, b))
    def _():
        wdb[...] = wd_ref[...].astype(BF16)

    def compute(rows):
        acc = jnp.dot(h_ref[0:rows, :], wdb[...], preferred_element_type=F32)
        half = acc.shape[1] // 2
        lo = pltpu.bitcast(acc[:, :half].astype(BF16).astype(F32), jnp.uint32) >> 16
        hi = pltpu.bitcast(acc[:, half:].astype(BF16).astype(F32), jnp.uint32) & jnp.uint32(0xFFFF0000)
        w = hi | lo
        for c in range(half // LANES):
            o_ref[0:rows, c, :] = w[:, c * LANES:(c + 1) * LANES]
        if rows < bm:
            o_ref[rows:, :, :] = jnp.zeros((bm - rows,) + o_ref.shape[1:], o_ref.dtype)

    _for_valid_rows(nv, bm, compute)

    @pl.when(nv == 0)
    def _():
        o_ref[...] = jnp.zeros(o_ref.shape, o_ref.dtype)


def _moe_experts(xs, w_gu, w_dn, layer, blk_e, blk_valid, n_used_blocks):
    rows, D = xs.shape
    F = w_gu.shape[-1] // 2
    bm = MOE_BM
    nb = rows // bm
    tf = _tile(F, MOE_TF, LANES)
    nf = F // tf
    td = _tile(D, max(256, (8 * 2**20) // (F * 4)), 256 if D % 256 == 0 else LANES)
    nd = D // td
    sl = td // 2 // LANES
    nu = n_used_blocks.reshape(1).astype(jnp.int32)

    def blk(b, nu_ref):
        return jnp.minimum(b, nu_ref[0] - 1)

    up_spec = pltpu.PrefetchScalarGridSpec(
        num_scalar_prefetch=3,
        grid=(nf, nb),
        in_specs=[
            pl.BlockSpec((bm, D), lambda f, b, be, nv, nu_: (blk(b, nu_), 0)),
            pl.BlockSpec((None, None, D, tf), lambda f, b, be, nv, nu_: (layer, be[blk(b, nu_)], 0, f)),
            pl.BlockSpec((None, None, D, tf), lambda f, b, be, nv, nu_: (layer, be[blk(b, nu_)], 0, nf + f)),
        ],
        out_specs=pl.BlockSpec((bm, tf), lambda f, b, be, nv, nu_: (b, f)),
        scratch_shapes=[pltpu.VMEM((D, tf), BF16), pltpu.VMEM((D, tf), BF16)],
    )
    vmem = 2 * bm * D * 2 + 4 * D * tf * 4 + 2 * D * tf * 2 + 2 * bm * tf * 2 + 6 * bm * tf * 4 + 4 * 2**20
    hidden = pl.pallas_call(
        _expert_up_kernel,
        grid_spec=up_spec,
        out_shape=jax.ShapeDtypeStruct((rows, F), BF16),
        compiler_params=_params(("arbitrary", "arbitrary"), vmem),
        name="moe_up",
    )(blk_e, blk_valid, nu, xs, w_gu, w_gu)

    down_spec = pltpu.PrefetchScalarGridSpec(
        num_scalar_prefetch=3,
        grid=(nd, nb),
        in_specs=[
            pl.BlockSpec((bm, F), lambda d, b, be, nv, nu_: (blk(b, nu_), 0)),
            pl.BlockSpec((None, None, F, td), lambda d, b, be, nv, nu_: (layer, be[blk(b, nu_)], 0, d)),
        ],
        out_specs=pl.BlockSpec((bm, None, sl, LANES), lambda d, b, be, nv, nu_: (b, d, 0, 0)),
        scratch_shapes=[pltpu.VMEM((F, td), BF16)],
    )
    vmem = 2 * bm * F * 2 + 2 * F * td * 4 + F * td * 2 + 2 * bm * td * 2 + 4 * bm * td * 4 + 4 * 2**20
    ys = pl.pallas_call(
        _expert_down_kernel,
        grid_spec=down_spec,
        out_shape=jax.ShapeDtypeStruct((rows, nd, sl, LANES), jnp.uint32),
        compiler_params=_params(("arbitrary", "arbitrary"), vmem),
        name="moe_down",
    )(blk_e, blk_valid, nu, hidden, w_dn)
    return ys.reshape(rows * nd * sl, LANES), td


def _combine_ln_kernel(pos_ref, pos_next_ref, x_ref, gt_ref, ys_hbm, g_ref, b_ref, o_ref, ob_ref, buf, sem,
                       *, tc, ns, pitch, td, head_blocks):
    i = pl.program_id(0)
    n = pl.num_programs(0)
    sl = td // 2 // LANES

    def copy(k, pos, slot, r):
        return pltpu.make_async_copy(ys_hbm.at[pl.ds(pos * ns, ns)], buf.at[slot, k, pl.ds(r * pitch, ns)],
                                     sem.at[slot])

    def issue(pref, slot):
        def body(r, c):
            for k in range(2):
                copy(k, pref[0, k * tc + r], slot, r).start(priority=k)
            return c

        lax.fori_loop(0, tc, body, 0, unroll=4)

    def consume(slot):
        def body(r, c):
            for k in range(2):
                copy(k, 0, slot, r).wait()
            return c

        lax.fori_loop(0, tc, body, 0, unroll=4)
        g0 = gt_ref[:, 0:1]
        g1 = gt_ref[:, 1:2]
        tiles = [None] * (2 * ns)
        for s in range(ns):
            d, c = divmod(s, sl)
            w0 = buf[slot, 0, pl.ds(s, tc, stride=pitch), :]
            w1 = buf[slot, 1, pl.ds(s, tc, stride=pitch), :]
            lo = pltpu.bitcast(w0 << 16, F32) * g0 + pltpu.bitcast(w1 << 16, F32) * g1
            hi = (pltpu.bitcast(w0 & jnp.uint32(0xFFFF0000), F32) * g0
                  + pltpu.bitcast(w1 & jnp.uint32(0xFFFF0000), F32) * g1)
            tiles[d * 2 * sl + c] = lo
            tiles[d * 2 * sl + sl + c] = hi
        y = DEEPNORM_ALPHA * x_ref[...] + jnp.concatenate(tiles, axis=1)
        out = _layer_norm_rows(y, g_ref[...], b_ref[...])
        if head_blocks is None:
            o_ref[...] = out
            ob_ref[...] = out.astype(BF16)
        else:
            @pl.when(i < head_blocks)
            def _():
                o_ref[...] = out

            @pl.when(i >= head_blocks)
            def _():
                ob_ref[...] = out

    for slot in range(2):
        @pl.when(i % 2 == slot)
        def _(slot=slot):
            if slot == 0:
                @pl.when(i == 0)
                def _():
                    issue(pos_ref, 0)

            @pl.when(i + 1 < n)
            def _():
                issue(pos_next_ref, 1 - slot)

            consume(slot)


def _moe_combine_ln(x, ys, td, dest, gate, g, b, head_rows=None):
    T, D = x.shape
    ns = D // 2 // LANES
    pitch = ns + SUBLANES
    tc = _tile(T, 256, SUBLANES) if head_rows is None else _tile(math.gcd(T, head_rows), 256, SUBLANES)
    nblk = T // tc
    pos = dest.reshape(2, nblk, tc).transpose(1, 0, 2).reshape(nblk, 1, 2 * tc)
    row = pl.BlockSpec((tc, D), lambda i: (i, 0))
    vec = pl.BlockSpec((1, D), lambda i: (0, 0))
    if head_rows is None:
        head_blocks = None
        out_specs = [row, row]
        out_shape = [jax.ShapeDtypeStruct((T, D), F32), jax.ShapeDtypeStruct((T, D), BF16)]
    else:
        head_blocks = head_rows // tc
        out_specs = [pl.BlockSpec((tc, D), lambda i: (jnp.minimum(i, head_blocks - 1), 0)),
                     pl.BlockSpec((tc, D), lambda i: (jnp.maximum(i - head_blocks, 0), 0))]
        out_shape = [jax.ShapeDtypeStruct((head_rows, D), F32), jax.ShapeDtypeStruct((T - head_rows, D), F32)]
    vmem = 4 * tc * pitch * LANES * 4 + 2 * tc * D * 4 + 2 * tc * D * (4 + 4) + 6 * tc * D * 4 + 4 * 2**20
    return pl.pallas_call(
        functools.partial(_combine_ln_kernel, tc=tc, ns=ns, pitch=pitch, td=td, head_blocks=head_blocks),
        grid=(nblk,),
        in_specs=[pl.BlockSpec((None, 1, 2 * tc), lambda i: (i, 0, 0), memory_space=pltpu.SMEM),
                  pl.BlockSpec((None, 1, 2 * tc), lambda i: (jnp.minimum(i + 1, nblk - 1), 0, 0),
                               memory_space=pltpu.SMEM),
                  row, pl.BlockSpec((tc, 2), lambda i: (i, 0)), pl.BlockSpec(memory_space=pl.ANY), vec, vec],
        out_specs=out_specs,
        out_shape=out_shape,
        scratch_shapes=[pltpu.VMEM((2, 2, tc * pitch, LANES), jnp.uint32), pltpu.SemaphoreType.DMA((2,))],
        compiler_params=_params(("arbitrary",), vmem),
        name="moe_combine_ln",
    )(pos, pos, x, gate.T, ys, g.reshape(1, D), b.reshape(1, D))


def _moe_layout(idx, T):
    E = N_EXPERTS
    bm = MOE_BM
    nb = -(-2 * T // bm) + E
    rows = nb * bm
    flat_e = idx.reshape(-1)
    tok = jnp.tile(jnp.arange(T, dtype=jnp.int32), 2)
    onehot = (flat_e[:, None] == jnp.arange(E, dtype=jnp.int32)[None, :]).astype(jnp.int32)
    csum = jnp.cumsum(onehot, axis=0)
    rank = jnp.sum(onehot * csum, axis=1) - 1
    counts = csum[-1]
    padded = (counts + bm - 1) // bm * bm
    pad_end = jnp.cumsum(padded)
    pad_start = pad_end - padded
    dest = (pad_start[flat_e] + rank).astype(jnp.int32)
    buf_tok = jnp.zeros((rows,), jnp.int32).at[dest].set(tok)
    blk_start = jnp.arange(nb, dtype=jnp.int32) * bm
    blk_e = jnp.minimum(jnp.searchsorted(pad_end, blk_start, side="right"), E - 1).astype(jnp.int32)
    blk_valid = jnp.clip(pad_start[blk_e] + counts[blk_e] - blk_start, 0, bm).astype(jnp.int32)
    n_used_rows = pad_end[-1].astype(jnp.int32)
    return dest, buf_tok, blk_e, blk_valid, n_used_rows, rows


def _routed_experts_ln(x, xb, xp, router_w, router_b, w_gu, w_dn, layer, g, b, head_rows=None):
    T, D = x.shape
    idx, gate = _router(xb, router_w, router_b)
    dest, buf_tok, blk_e, blk_valid, n_used_rows, rows = _moe_layout(idx, T)
    xs = _moe_gather(xp, buf_tok, n_used_rows, rows, D)
    ys, td = _moe_experts(xs, w_gu, w_dn, layer, blk_e, blk_valid, n_used_rows // MOE_BM)
    return _moe_combine_ln(x, ys, td, dest, gate, g, b, head_rows)


def kernel(x_prompt, x_sample, mem_prompt, mem_sample, da_w_qkv, da_lam, da_subln_g, da_w_o, sg_w_in, sg_b_in, sg_ln_g, sg_ln_b, sg_w_s, sg_b_s, sg_w_o, rg_w_in, rg_conv_w, rg_conv_b, rg_w_a, rg_b_a, rg_w_i, rg_b_i, rg_lambda, rg_w_o, ca_w_q, ca_w_kv, ca_w_o, router_w, router_b, moe_w_gu, moe_w_dn, ln_g, ln_b):
    Bp, S, D = x_prompt.shape
    Bs = x_sample.shape[0]
    assert x_sample.shape[1] == S, "both request groups must share the sequence length"
    B = Bp + Bs
    M = mem_prompt.shape[1]
    x = jnp.concatenate([x_prompt.reshape(Bp * S, D), x_sample.reshape(Bs * S, D)], axis=0)
    xb = x.astype(BF16)
    mem = jnp.concatenate([mem_prompt.reshape(Bp * M, D), mem_sample.reshape(Bs * M, D)], axis=0).astype(BF16)
    slopes = jnp.asarray([2.0 ** (-8.0 * (h + 1) / DA_HEADS) for h in range(DA_HEADS)], F32)
    dh = D // (2 * DA_HEADS)
    qkv_scale = jnp.concatenate([jnp.full((D,), LOG2E * dh ** -0.5, F32), jnp.ones((2 * D,), F32)])

    for i in range(DEPTH):
        kind, slot = i % N_MIXERS, i // N_MIXERS
        if kind == 0:
            lam_init = 0.8 - 0.6 * math.exp(-0.3 * i)
            lf = da_lam[slot].astype(F32)
            lam_full = (jnp.exp(jnp.sum(lf[0] * lf[1])) - jnp.exp(jnp.sum(lf[2] * lf[3])) + lam_init).reshape(1)
            qkv = _mm(xb, da_w_qkv, slot, col_scale=qkv_scale, out_dtype=BF16, name="da_qkv")
            o = _diff_attention(qkv, B, S, slopes, lam_full, da_subln_g[slot], lam_init)
            h = _mm(o, da_w_o, slot, out_dtype=BF16, name="da_out")
        elif kind == 1:
            z = _mm(xb, sg_w_in.astype(BF16), slot, bias=sg_b_in[slot], act="gelu", out_dtype=BF16, name="sg_in")
            gated = _spatial_gate(z, sg_ln_g[slot], sg_ln_b[slot], sg_w_s[slot], sg_b_s[slot])
            h = _mm(gated, sg_w_o.astype(BF16), slot, out_dtype=BF16, name="sg_out")
        else:
            xg = _mm(xb, rg_w_in, slot, name="rg_in")
            hr = _rglru_core(xg, B, S, rg_conv_w[slot], rg_conv_b[slot], rg_w_a[slot], rg_b_a[slot],
                             rg_w_i[slot], rg_b_i[slot], rg_lambda[slot])
            h = _mm(hr, rg_w_o, slot, out_dtype=BF16, name="rg_out")
        x, xb = _add_ln(x, h, ln_g[i, 0], ln_b[i, 0])

        q = _mm(xb, ca_w_q, i, out_dtype=BF16, name="ca_q")
        kv = _mm(mem, ca_w_kv, i, out_dtype=BF16, name="ca_kv")
        o = _memory_attention_core(q, kv, B, S, M)
        h = _mm(o, ca_w_o, i, out_dtype=BF16, name="ca_out")
        x, xb, xp = _add_ln(x, h, ln_g[i, 1], ln_b[i, 1], emit_packed=True)

        x, xb = _routed_experts_ln(x, xb, xp, router_w, router_b, moe_w_gu, moe_w_dn, i, ln_g[i, 2], ln_b[i, 2],
                                   head_rows=Bp * S if i == DEPTH - 1 else None)

    return (x.reshape(Bp, S, D), xb.reshape(Bs, S, D))
```

```python
import functools
import math

import jax
import jax.numpy as jnp
from jax import lax
from jax.experimental import pallas as pl
from jax.experimental.pallas import tpu as pltpu

DEPTH = 4
N_MIXERS = 3
DA_HEADS = 16
SG_CHUNK = 128
SG_GROUPS = 16
RG_BLOCKS = 16
RG_C = 8.0
CA_HEADS = 4
N_EXPERTS = 32
N_GROUPS = 8
LN_EPS = 1e-5
DEEPNORM_ALPHA = (2 * DEPTH) ** 0.25
LOG2E = math.log2(math.e)

V7X_VMEM_BYTES = 64 * 1024 * 1024
VMEM_CAP = V7X_VMEM_BYTES - 6 * 1024 * 1024
SUBLANES = 8
LANES = 128

MOE_BM = 512
MOE_SUB = 128
MOE_TF = 512
MOE_TG = 512

BF16 = jnp.bfloat16
F32 = jnp.float32


def _tile(dim, pref, align):
    if dim <= pref:
        return dim
    t = (pref // align) * align
    while t > align and dim % t:
        t -= align
    assert dim % t == 0, (dim, pref, align)
    return t


def _params(sem, vmem_bytes):
    return pltpu.CompilerParams(dimension_semantics=sem,
                                vmem_limit_bytes=int(min(max(vmem_bytes, 16 * 2**20), VMEM_CAP)))


def _mm_kernel(*refs, act, has_bias, has_scale, cast_w):
    x_ref, w_ref = refs[0], refs[1]
    if cast_w:
        o_ref, wbf_ref = refs[-2], refs[-1]
        extra = list(refs[2:-2])

        @pl.when(pl.program_id(1) == 0)
        def _():
            wbf_ref[...] = w_ref[...].astype(BF16)
    else:
        o_ref, wbf_ref = refs[-1], w_ref
        extra = list(refs[2:-1])

    acc = jnp.dot(x_ref[...].astype(BF16), wbf_ref[...], preferred_element_type=F32)
    if has_bias:
        acc = acc + extra.pop(0)[...]
    if has_scale:
        acc = acc * extra.pop(0)[...]
    if act == "gelu":
        acc = jax.nn.gelu(acc)
    o_ref[...] = acc.astype(o_ref.dtype)


def _mm(x, w, layer, bias=None, col_scale=None, act=None, out_dtype=F32, name="mm"):
    M, K = x.shape
    _, K2, N = w.shape
    assert K == K2
    xb = x.dtype.itemsize
    wb = w.dtype.itemsize
    cast_w = w.dtype != BF16
    ob = jnp.dtype(out_dtype).itemsize
    x_budget = (12 if cast_w else 8) * 2**20
    tm = _tile(M, max(256, x_budget // (K * xb)), 256 if M % 256 == 0 else SUBLANES)
    tn = _tile(N, max(256, (8 * 2**20) // (K * wb)), 256 if N % 256 == 0 else LANES)
    grid = (N // tn, M // tm)
    in_specs = [pl.BlockSpec((tm, K), lambda j, i: (i, 0)),
                pl.BlockSpec((None, K, tn), lambda j, i: (layer, 0, j))]
    args = [x, w]
    for vec in (bias, col_scale):
        if vec is not None:
            in_specs.append(pl.BlockSpec((1, tn), lambda j, i: (0, j)))
            args.append(vec.reshape(1, N).astype(F32))
    vmem = (2 * tm * K * xb + 2 * K * tn * wb + (K * tn * 2 if cast_w else 0) + 2 * tm * tn * ob
            + 2 * tm * tn * 4 + 4 * 2**20)
    return pl.pallas_call(
        functools.partial(_mm_kernel, act=act, has_bias=bias is not None, has_scale=col_scale is not None,
                          cast_w=cast_w),
        grid=grid,
        in_specs=in_specs,
        out_specs=pl.BlockSpec((tm, tn), lambda j, i: (i, j)),
        out_shape=jax.ShapeDtypeStruct((M, N), out_dtype),
        scratch_shapes=[pltpu.VMEM((K, tn), BF16)] if cast_w else [],
        compiler_params=_params(("parallel", "arbitrary"), vmem),
        name=name,
    )(*args)


def _layer_norm_rows(y, g, b):
    mu = jnp.mean(y, axis=-1, keepdims=True)
    yc = y - mu
    var = jnp.mean(yc * yc, axis=-1, keepdims=True)
    return yc * lax.rsqrt(var + LN_EPS) * g + b


def _pack_slabs(out, xp_ref):
    tm, D = out.shape
    half = D // 2
    ns = half // LANES
    lo = pltpu.bitcast(out[:, :half].astype(BF16).astype(F32), jnp.uint32) >> 16
    hi = pltpu.bitcast(out[:, half:].astype(BF16).astype(F32), jnp.uint32) & jnp.uint32(0xFFFF0000)
    w = hi | lo
    for c in range(ns):
        xp_ref[pl.ds(c, tm, stride=ns), :] = w[:, c * LANES:(c + 1) * LANES]


def _add_ln_kernel(x_ref, h_ref, g_ref, b_ref, o_ref, ob_ref, *maybe_xp_ref):
    y = DEEPNORM_ALPHA * x_ref[...] + h_ref[...].astype(F32)
    out = _layer_norm_rows(y, g_ref[...], b_ref[...])
    o_ref[...] = out
    ob_ref[...] = out.astype(BF16)
    if maybe_xp_ref:
        _pack_slabs(out, maybe_xp_ref[0])


def _add_ln(x, h, g, b, emit_packed=False):
    T, D = x.shape
    tm = _tile(T, 256, SUBLANES)
    ns = D // 2 // LANES
    row = pl.BlockSpec((tm, D), lambda i: (i, 0))
    vec = pl.BlockSpec((1, D), lambda i: (0, 0))
    out_specs = [row, row]
    out_shape = [jax.ShapeDtypeStruct((T, D), F32), jax.ShapeDtypeStruct((T, D), BF16)]
    if emit_packed:
        out_specs.append(pl.BlockSpec((tm * ns, LANES), lambda i: (i, 0)))
        out_shape.append(jax.ShapeDtypeStruct((T * ns, LANES), jnp.uint32))
    vmem = 2 * tm * D * (4 + h.dtype.itemsize + 4 + 2 + 2) + 6 * tm * D * 4
    return pl.pallas_call(
        _add_ln_kernel,
        grid=(T // tm,),
        in_specs=[row, row, vec, vec],
        out_specs=out_specs,
        out_shape=out_shape,
        compiler_params=_params(("parallel",), vmem),
        name="add_ln",
    )(x, h, g.reshape(1, D), b.reshape(1, D))


def _da_kernel(slope_ref, lam_ref, q_ref, k_ref, v_ref, g_ref, o_ref, bias_ref, *, tq, sub, lam_init):
    h = pl.program_id(0)
    qi = pl.program_id(1)
    S = k_ref.shape[0]
    dh = q_ref.shape[1] // 2

    @pl.when(pl.program_id(2) == 0)
    def _():
        qpos = qi * tq + lax.broadcasted_iota(jnp.int32, (tq, S), 0)
        kpos = lax.broadcasted_iota(jnp.int32, (tq, S), 1)
        bias_ref[...] = jnp.abs(qpos - kpos).astype(F32) * (-LOG2E * slope_ref[h])

    lam = lam_ref[0]
    k = k_ref[...]
    v = v_ref[...]

    def unnormalised(q_rows, rows, c):
        s = lax.dot_general(q_rows[:, c * dh:(c + 1) * dh], k[:, c * dh:(c + 1) * dh],
                            (((1,), (1,)), ((), ())), preferred_element_type=F32)
        xs = [s[:, j * LANES:(j + 1) * LANES] + bias_ref[rows, j * LANES:(j + 1) * LANES]
              for j in range(S // LANES)]
        m = functools.reduce(jnp.maximum, xs)
        m = jnp.max(m, axis=-1, keepdims=True)
        es = [jnp.exp2(x - m) for x in xs]
        l = jnp.sum(functools.reduce(lambda p, t: p + t, es), axis=-1, keepdims=True)
        e = jnp.concatenate([t.astype(BF16) for t in es], axis=1)
        return jnp.dot(e, v, preferred_element_type=F32), l

    for r0 in range(0, tq, sub):
        rows = slice(r0, r0 + sub)
        q_rows = q_ref[rows, :]
        pv1, l1 = unnormalised(q_rows, rows, 0)
        pv2, l2 = unnormalised(q_rows, rows, 1)
        o = pv1 * (1.0 / l1) - pv2 * (lam / l2)
        ms = jnp.mean(o * o, axis=-1, keepdims=True)
        o = o * lax.rsqrt(ms + LN_EPS) * g_ref[...] * (1.0 - lam_init)
        o_ref[rows, :] = o.astype(o_ref.dtype)


def _diff_attention(qkv, B, S, slopes, lam_full, subln_g, lam_init):
    T, D3 = qkv.shape
    D = D3 // 3
    H = DA_HEADS
    hw = D // H
    tq = _tile(S, 1024, SUBLANES)
    nq = S // tq
    smem = pl.BlockSpec(memory_space=pltpu.SMEM)
    vmem = 10 * tq * S * 4 + 8 * S * hw * 2 + 4 * 2**20
    return pl.pallas_call(
        functools.partial(_da_kernel, tq=tq, sub=_tile(tq, 128, SUBLANES), lam_init=lam_init),
        grid=(H, nq, B),
        in_specs=[smem, smem,
                  pl.BlockSpec((tq, hw), lambda h, qi, b: (b * nq + qi, h)),
                  pl.BlockSpec((S, hw), lambda h, qi, b: (b, H + h)),
                  pl.BlockSpec((S, hw), lambda h, qi, b: (b, 2 * H + h)),
                  pl.BlockSpec((1, hw), lambda h, qi, b: (0, 0))],
        out_specs=pl.BlockSpec((tq, hw), lambda h, qi, b: (b * nq + qi, h)),
        out_shape=jax.ShapeDtypeStruct((T, D), BF16),
        scratch_shapes=[pltpu.VMEM((tq, S), F32)],
        compiler_params=_params(("parallel", "parallel", "arbitrary"), vmem),
        name="diff_attn",
    )(slopes, lam_full, qkv, qkv, qkv, subln_g.reshape(1, hw).astype(F32))


def _ca_kernel(q_ref, k_ref, v_ref, o_ref):
    dh = q_ref.shape[1]
    s = lax.dot_general(q_ref[...], k_ref[...], (((1,), (1,)), ((), ())),
                        preferred_element_type=F32) * (dh ** -0.5)
    m = jnp.max(s, axis=-1, keepdims=True)
    e = jnp.exp(s - m)
    p = e * (1.0 / jnp.sum(e, axis=-1, keepdims=True))
    o_ref[...] = jnp.dot(p.astype(BF16), v_ref[...], preferred_element_type=F32).astype(o_ref.dtype)


def _memory_attention_core(q, kv, B, S, M):
    T, D = q.shape
    H = CA_HEADS
    dh = D // H
    tq = _tile(S, 2048, SUBLANES)
    nq = S // tq
    vmem = 4 * tq * dh * 2 * 2 + 4 * M * dh * 2 + 4 * tq * M * 4 + 2 * tq * dh * 4 + 4 * 2**20
    return pl.pallas_call(
        _ca_kernel,
        grid=(B, H, nq),
        in_specs=[pl.BlockSpec((tq, dh), lambda b, h, qi: (b * nq + qi, h)),
                  pl.BlockSpec((M, dh), lambda b, h, qi: (b, h)),
                  pl.BlockSpec((M, dh), lambda b, h, qi: (b, H + h))],
        out_specs=pl.BlockSpec((tq, dh), lambda b, h, qi: (b * nq + qi, h)),
        out_shape=jax.ShapeDtypeStruct((T, D), BF16),
        compiler_params=_params(("parallel", "parallel", "parallel"), vmem),
        name="mem_attn",
    )(q, kv, kv)


def _sg_kernel(u_ref, v_ref, g_ref, b_ref, ws_ref, bs_ref, o_ref, *, n_chunks):
    W = v_ref.shape[1]
    G = ws_ref.shape[0]
    gw = W // G
    vn = _layer_norm_rows(v_ref[...].astype(F32), g_ref[...], b_ref[...]).astype(BF16)
    for c in range(n_chunks):
        rows = slice(c * SG_CHUNK, (c + 1) * SG_CHUNK)
        for g in range(G):
            cols = slice(g * gw, (g + 1) * gw)
            y = jnp.dot(ws_ref[g], vn[rows, cols], preferred_element_type=F32) + bs_ref[:, g:g + 1]
            o_ref[rows, cols] = (u_ref[rows, cols].astype(F32) * y).astype(o_ref.dtype)


def _spatial_gate(z, ln_g, ln_b, w_s, b_s):
    T, W2 = z.shape
    W = W2 // 2
    n_chunks = 2 if T % (2 * SG_CHUNK) == 0 else 1
    R = n_chunks * SG_CHUNK
    G = w_s.shape[0]
    zb = z.dtype.itemsize
    vmem = 4 * R * W * zb + 2 * R * W * 2 + 4 * R * W * 4 + 4 * 2**20
    return pl.pallas_call(
        functools.partial(_sg_kernel, n_chunks=n_chunks),
        grid=(T // R,),
        in_specs=[pl.BlockSpec((R, W), lambda i: (i, 0)),
                  pl.BlockSpec((R, W), lambda i: (i, 1)),
                  pl.BlockSpec((1, W), lambda i: (0, 0)),
                  pl.BlockSpec((1, W), lambda i: (0, 0)),
                  pl.BlockSpec((G, SG_CHUNK, SG_CHUNK), lambda i: (0, 0, 0)),
                  pl.BlockSpec((SG_CHUNK, G), lambda i: (0, 0))],
        out_specs=pl.BlockSpec((R, W), lambda i: (i, 0)),
        out_shape=jax.ShapeDtypeStruct((T, W), BF16),
        compiler_params=_params(("parallel",), vmem),
        name="spatial_gate",
    )(z, z, ln_g.reshape(1, W), ln_b.reshape(1, W), w_s.astype(BF16), b_s.T.astype(F32))


def _rg_kernel(xr_ref, g_ref, cw_ref, cb_ref, w4_ref, b4_ref, c_ref, o_ref, a_s, b_s, h_s):
    S, C = xr_ref.shape
    xr = xr_ref[...].astype(F32)
    t = lax.broadcasted_iota(jnp.int32, (S, C), 0)
    x_m2 = jnp.where(t >= 2, pltpu.roll(xr, 2, 0), 0.0)
    x_m1 = jnp.where(t >= 1, pltpu.roll(xr, 1, 0), 0.0)
    x_p1 = jnp.where(t < S - 1, pltpu.roll(xr, S - 1, 0), 0.0)
    xc = x_m2 * cw_ref[0:1, :] + x_m1 * cw_ref[1:2, :] + xr * cw_ref[2:3, :] + x_p1 * cw_ref[3:4, :] + cb_ref[...]
    pre = jnp.dot(xc.astype(BF16), w4_ref[...], preferred_element_type=F32) + b4_ref[...]
    def sigmoid(z):
        return 0.5 * jnp.tanh(0.5 * z) + 0.5

    for d in range(2):
        r = sigmoid(pre[:, (2 * d) * C:(2 * d + 1) * C])
        i = sigmoid(pre[:, (2 * d + 1) * C:(2 * d + 2) * C])
        a = jnp.exp(c_ref[d:d + 1, :] * r)
        a_s[d] = a
        y = jnp.maximum(1.0 - a * a, 0.0)
        root = jnp.where(y > 0.0, y * lax.rsqrt(y), 0.0)
        b_s[d] = root * (i * xc)

    row = lax.broadcasted_iota(jnp.int32, (SUBLANES, C), 0)
    n_groups = S // SUBLANES

    def group_scan(A, Bv, reverse):
        for sh in (1, 2, 4):
            if reverse:
                ok = row < SUBLANES - sh
                rs = SUBLANES - sh
            else:
                ok = row >= sh
                rs = sh
            A_sh = jnp.where(ok, pltpu.roll(A, rs, 0), 1.0)
            B_sh = jnp.where(ok, pltpu.roll(Bv, rs, 0), 0.0)
            Bv = A * B_sh + Bv
            A = A * A_sh
        return A, Bv

    def body(gi, carry):
        hf, hb = carry
        r0 = pl.multiple_of(gi * SUBLANES, SUBLANES)
        A, Bv = group_scan(a_s[0, pl.ds(r0, SUBLANES), :], b_s[0, pl.ds(r0, SUBLANES), :], False)
        Hf = A * hf + Bv
        h_s[0, pl.ds(r0, SUBLANES), :] = Hf
        r1 = pl.multiple_of((n_groups - 1 - gi) * SUBLANES, SUBLANES)
        A, Bv = group_scan(a_s[1, pl.ds(r1, SUBLANES), :], b_s[1, pl.ds(r1, SUBLANES), :], True)
        Hb = A * hb + Bv
        h_s[1, pl.ds(r1, SUBLANES), :] = Hb
        return Hf[SUBLANES - 1:SUBLANES, :], Hb[0:1, :]

    zero = jnp.zeros((1, C), F32)
    lax.fori_loop(0, n_groups, body, (zero, zero), unroll=8)
    gate = jax.nn.gelu(g_ref[...].astype(F32))
    o_ref[...] = ((h_s[0] + h_s[1]) * gate).astype(o_ref.dtype)


def _rglru_core(xg, B, S, conv_w, conv_b, w_a, b_a, w_i, b_i, lam):
    T, R2 = xg.shape
    R = R2 // 2
    nb = RG_BLOCKS
    C = R // nb
    w4 = jnp.concatenate([w_a[0], w_i[0], w_a[1], w_i[1]], axis=-1).astype(BF16)
    b4 = jnp.stack([b_a[0].reshape(nb, C), b_i[0].reshape(nb, C), b_a[1].reshape(nb, C), b_i[1].reshape(nb, C)],
                   axis=1).reshape(nb, 1, 4 * C).astype(F32)
    coef = (-RG_C) * jax.nn.softplus(-lam.astype(F32))
    xb = xg.dtype.itemsize
    vmem = 4 * S * C * xb + 2 * S * C * 2 + 4 * S * C * 4 + 3 * S * 4 * C * 4 + 8 * S * C * 4 + 4 * 2**20
    return pl.pallas_call(
        _rg_kernel,
        grid=(B, nb),
        in_specs=[pl.BlockSpec((S, C), lambda b, n: (b, n)),
                  pl.BlockSpec((S, C), lambda b, n: (b, nb + n)),
                  pl.BlockSpec((4, C), lambda b, n: (0, n)),
                  pl.BlockSpec((1, C), lambda b, n: (0, n)),
                  pl.BlockSpec((None, C, 4 * C), lambda b, n: (n, 0, 0)),
                  pl.BlockSpec((None, 1, 4 * C), lambda b, n: (n, 0, 0)),
                  pl.BlockSpec((2, C), lambda b, n: (0, n))],
        out_specs=pl.BlockSpec((S, C), lambda b, n: (b, n)),
        out_shape=jax.ShapeDtypeStruct((T, R), BF16),
        scratch_shapes=[pltpu.VMEM((2, S, C), F32)] * 3,
        compiler_params=_params(("parallel", "parallel"), vmem),
        name="rglru_core",
    )(xg, xg, conv_w.astype(F32), conv_b.reshape(1, R).astype(F32), w4, b4, coef)


def _first_top2(vals):
    m1 = vals[0]
    j1 = jnp.zeros(vals[0].shape, jnp.int32)
    for j in range(1, len(vals)):
        upd = vals[j] > m1
        m1 = jnp.where(upd, vals[j], m1)
        j1 = jnp.where(upd, j, j1)
    m2 = jnp.full(vals[0].shape, -jnp.inf, F32)
    j2 = jnp.zeros(vals[0].shape, jnp.int32)
    for j in range(len(vals)):
        upd = (j1 != j) & (vals[j] > m2)
        m2 = jnp.where(upd, vals[j], m2)
        j2 = jnp.where(upd, j, j2)
    return m1, j1, m2, j2


def _router_kernel(x_ref, rw_ref, rb_ref, idx_ref, gate_ref):
    G = N_GROUPS
    J = N_EXPERTS // G
    logits = lax.dot_general(rw_ref[...], x_ref[...], (((1,), (1,)), ((), ())), preferred_element_type=F32)
    s = jax.nn.sigmoid(logits)
    biased = s + rb_ref[...]
    a = [biased[j * G:(j + 1) * G, :] for j in range(J)]
    sv = [s[j * G:(j + 1) * G, :] for j in range(J)]
    m1, _, m2, _ = _first_top2(a)
    gscore = m1 + m2
    gidx = lax.broadcasted_iota(jnp.int32, gscore.shape, 0)
    gmax = jnp.max(gscore, axis=0, keepdims=True)
    gsel = jnp.min(jnp.where(gscore == gmax, gidx, G), axis=0, keepdims=True)
    mask = gidx == gsel
    av = [jnp.sum(jnp.where(mask, a[j], 0.0), axis=0, keepdims=True) for j in range(J)]
    sw = [jnp.sum(jnp.where(mask, sv[j], 0.0), axis=0, keepdims=True) for j in range(J)]
    _, j1, _, j2 = _first_top2(av)
    w1 = jnp.zeros_like(sw[0])
    w2 = jnp.zeros_like(sw[0])
    for j in range(J):
        w1 = jnp.where(j1 == j, sw[j], w1)
        w2 = jnp.where(j2 == j, sw[j], w2)
    tot = w1 + w2
    idx_ref[0:1, :] = gsel * J + j1
    idx_ref[1:2, :] = gsel * J + j2
    gate_ref[0:1, :] = w1 / tot
    gate_ref[1:2, :] = w2 / tot


def _router(xb, router_w, router_b):
    T, D = xb.shape
    G = N_GROUPS
    J = N_EXPERTS // G
    perm = jnp.asarray([(r % G) * J + r // G for r in range(N_EXPERTS)], jnp.int32)
    rw = router_w.T[perm].astype(BF16)
    rb = router_b[perm].reshape(N_EXPERTS, 1).astype(F32)
    tm = _tile(T, 1024, LANES)
    vmem = 2 * tm * D * 2 + 2 * N_EXPERTS * D * 2 + 64 * tm * 4 * 4 + 4 * 2**20
    return pl.pallas_call(
        _router_kernel,
        grid=(T // tm,),
        in_specs=[pl.BlockSpec((tm, D), lambda i: (i, 0)),
                  pl.BlockSpec((N_EXPERTS, D), lambda i: (0, 0)),
                  pl.BlockSpec((N_EXPERTS, 1), lambda i: (0, 0))],
        out_specs=[pl.BlockSpec((2, tm), lambda i: (0, i)), pl.BlockSpec((2, tm), lambda i: (0, i))],
        out_shape=[jax.ShapeDtypeStruct((2, T), jnp.int32), jax.ShapeDtypeStruct((2, T), F32)],
        compiler_params=_params(("parallel",), vmem),
        name="router",
    )(xb, rw, rb)


def _slab_copy(xp_hbm, tok, buf, slot, r, sem, ns, pitch):
    return pltpu.make_async_copy(xp_hbm.at[pl.ds(tok * ns, ns)], buf.at[slot, pl.ds(r * pitch, ns)], sem.at[slot])


def _gather_kernel(nused_ref, tok_ref, tok_next_ref, xp_hbm, o_ref, buf, sem, *, tg, ns, pitch):
    i = pl.program_id(0)
    nu = nused_ref[0]
    half = ns * LANES

    def issue(tref, slot):
        def body(r2, c):
            for p in range(2):
                r = 2 * r2 + p
                _slab_copy(xp_hbm, tref[0, r], buf, slot, r, sem, ns, pitch).start(priority=p)
            return c

        lax.fori_loop(0, tg // 2, body, 0, unroll=4)

    def consume(slot):
        def body(r, c):
            _slab_copy(xp_hbm, 0, buf, slot, r, sem, ns, pitch).wait()
            return c

        lax.fori_loop(0, tg, body, 0, unroll=8)
        for c in range(ns):
            w = buf[slot, pl.ds(c, tg, stride=pitch), :]
            lo = pltpu.bitcast(w << 16, F32)
            hi = pltpu.bitcast(w & jnp.uint32(0xFFFF0000), F32)
            o_ref[:, c * LANES:(c + 1) * LANES] = lo.astype(o_ref.dtype)
            o_ref[:, half + c * LANES:half + (c + 1) * LANES] = hi.astype(o_ref.dtype)

    for slot in range(2):
        @pl.when((i < nu) & (i % 2 == slot))
        def _(slot=slot):
            if slot == 0:
                @pl.when(i == 0)
                def _():
                    issue(tok_ref, 0)

            @pl.when(i + 1 < nu)
            def _():
                issue(tok_next_ref, 1 - slot)

            consume(slot)

    @pl.when(i >= nu)
    def _():
        o_ref[...] = jnp.zeros(o_ref.shape, o_ref.dtype)


def _moe_gather(xp, buf_tok, n_used_rows, rows, D):
    ns = D // 2 // LANES
    pitch = ns + SUBLANES
    tg = MOE_TG
    nblk = rows // tg
    tok3 = buf_tok.reshape(nblk, 1, tg)
    grid_spec = pltpu.PrefetchScalarGridSpec(
        num_scalar_prefetch=1,
        grid=(nblk,),
        in_specs=[pl.BlockSpec((None, 1, tg), lambda i, nu: (i, 0, 0), memory_space=pltpu.SMEM),
                  pl.BlockSpec((None, 1, tg), lambda i, nu: (jnp.minimum(i + 1, nblk - 1), 0, 0),
                               memory_space=pltpu.SMEM),
                  pl.BlockSpec(memory_space=pl.ANY)],
        out_specs=pl.BlockSpec((tg, D), lambda i, nu: (i, 0)),
        scratch_shapes=[pltpu.VMEM((2, tg * pitch, LANES), jnp.uint32), pltpu.SemaphoreType.DMA((2,))],
    )
    vmem = 2 * tg * pitch * LANES * 4 + 2 * tg * D * 2 + 4 * tg * D * 4 + 4 * 2**20
    return pl.pallas_call(
        functools.partial(_gather_kernel, tg=tg, ns=ns, pitch=pitch),
        grid_spec=grid_spec,
        out_shape=jax.ShapeDtypeStruct((rows, D), BF16),
        compiler_params=_params(("arbitrary",), vmem),
        name="moe_gather",
    )((n_used_rows // tg).reshape(1).astype(jnp.int32), tok3, tok3, xp)


def _expert_changed(be_ref, b):
    return (b == 0) | (be_ref[b] != be_ref[jnp.maximum(b - 1, 0)])


def _for_valid_rows(nv, bm, fn):
    nsub = (nv + MOE_SUB - 1) // MOE_SUB
    for k in range(1, bm // MOE_SUB + 1):
        @pl.when(nsub == k)
        def _(k=k):
            fn(k * MOE_SUB)


def _expert_up_kernel(be_ref, nv_ref, nu_ref, x_ref, wg_ref, wu_ref, h_ref, wgb, wub):
    del nu_ref
    b = pl.program_id(1)
    nv = nv_ref[b]
    bm = x_ref.shape[0]

    @pl.when((nv > 0) & _expert_changed(be_ref, b))
    def _():
        wgb[...] = wg_ref[...].astype(BF16)
        wub[...] = wu_ref[...].astype(BF16)

    def compute(rows):
        x = x_ref[0:rows, :]
        g = jnp.dot(x, wgb[...], preferred_element_type=F32)
        u = jnp.dot(x, wub[...], preferred_element_type=F32)
        h_ref[0:rows, :] = (jax.nn.silu(g) * u).astype(h_ref.dtype)
        if rows < bm:
            h_ref[rows:, :] = jnp.zeros((bm - rows, h_ref.shape[1]), h_ref.dtype)

    _for_valid_rows(nv, bm, compute)

    @pl.when(nv == 0)
    def _():
        h_ref[...] = jnp.zeros(h_ref.shape, h_ref.dtype)


def _expert_down_kernel(be_ref, nv_ref, nu_ref, h_ref, wd_ref, o_ref, wdb):
    del nu_ref
    b = pl.program_id(1)
    nv = nv_ref[b]
    bm = h_ref.shape[0]

    @pl.when((nv > 0) & _expert_changed(be_ref, b))
    def _():
        wdb[...] = wd_ref[...].astype(BF16)

    def compute(rows):
        hblk = h_ref[0:rows, :]
        for c in range(o_ref.shape[1]):
            acc = jnp.dot(hblk, wdb[:, 2 * c * LANES:2 * (c + 1) * LANES], preferred_element_type=F32)
            lo = pltpu.bitcast(acc[:, :LANES].astype(BF16).astype(F32), jnp.uint32) >> 16
            hi = pltpu.bitcast(acc[:, LANES:].astype(BF16).astype(F32), jnp.uint32) & jnp.uint32(0xFFFF0000)
            o_ref[0:rows, c, :] = hi | lo
        if rows < bm:
            o_ref[rows:, :, :] = jnp.zeros((bm - rows,) + o_ref.shape[1:], o_ref.dtype)

    _for_valid_rows(nv, bm, compute)

    @pl.when(nv == 0)
    def _():
        o_ref[...] = jnp.zeros(o_ref.shape, o_ref.dtype)


def _moe_experts(xs, w_gu, w_dn, layer, blk_e, blk_valid, n_used_blocks):
    rows, D = xs.shape
    F = w_gu.shape[-1] // 2
    bm = MOE_BM
    nb = rows // bm
    tf = _tile(F, MOE_TF, LANES)
    nf = F // tf
    td = _tile(D, max(256, (8 * 2**20) // (F * 4)), 256 if D % 256 == 0 else LANES)
    nd = D // td
    sl = td // 2 // LANES
    nu = n_used_blocks.reshape(1).astype(jnp.int32)

    def blk(b, nu_ref):
        return jnp.minimum(b, nu_ref[0] - 1)

    up_spec = pltpu.PrefetchScalarGridSpec(
        num_scalar_prefetch=3,
        grid=(nf, nb),
        in_specs=[
            pl.BlockSpec((bm, D), lambda f, b, be, nv, nu_: (blk(b, nu_), 0)),
            pl.BlockSpec((None, None, D, tf), lambda f, b, be, nv, nu_: (layer, be[blk(b, nu_)], 0, f)),
            pl.BlockSpec((None, None, D, tf), lambda f, b, be, nv, nu_: (layer, be[blk(b, nu_)], 0, nf + f)),
        ],
        out_specs=pl.BlockSpec((bm, tf), lambda f, b, be, nv, nu_: (b, f)),
        scratch_shapes=[pltpu.VMEM((D, tf), BF16), pltpu.VMEM((D, tf), BF16)],
    )
    vmem = 2 * bm * D * 2 + 4 * D * tf * 4 + 2 * D * tf * 2 + 2 * bm * tf * 2 + 6 * bm * tf * 4 + 4 * 2**20
    hidden = pl.pallas_call(
        _expert_up_kernel,
        grid_spec=up_spec,
        out_shape=jax.ShapeDtypeStruct((rows, F), BF16),
        compiler_params=_params(("arbitrary", "arbitrary"), vmem),
        name="moe_up",
    )(blk_e, blk_valid, nu, xs, w_gu, w_gu)

    down_spec = pltpu.PrefetchScalarGridSpec(
        num_scalar_prefetch=3,
        grid=(nd, nb),
        in_specs=[
            pl.BlockSpec((bm, F), lambda d, b, be, nv, nu_: (blk(b, nu_), 0)),
            pl.BlockSpec((None, None, F, td), lambda d, b, be, nv, nu_: (layer, be[blk(b, nu_)], 0, d)),
        ],
        out_specs=pl.BlockSpec((bm, None, sl, LANES), lambda d, b, be, nv, nu_: (b, d, 0, 0)),
        scratch_shapes=[pltpu.VMEM((F, td), BF16)],
    )
    vmem = 2 * bm * F * 2 + 2 * F * td * 4 + F * td * 2 + 2 * bm * td * 2 + 4 * bm * td * 4 + 4 * 2**20
    ys = pl.pallas_call(
        _expert_down_kernel,
        grid_spec=down_spec,
        out_shape=jax.ShapeDtypeStruct((rows, nd, sl, LANES), jnp.uint32),
        compiler_params=_params(("arbitrary", "arbitrary"), vmem),
        name="moe_down",
    )(blk_e, blk_valid, nu, hidden, w_dn)
    return ys.reshape(rows * nd * sl, LANES), td


def _combine_ln_kernel(pos_ref, pos_next_ref, x_ref, gt_ref, ys_hbm, g_ref, b_ref, o_ref, ob_ref, buf, sem,
                       *, tc, ns, pitch, td, head_blocks):
    del td
    i = pl.program_id(0)
    n = pl.num_programs(0)

    def copy(k, pos, slot, r):
        return pltpu.make_async_copy(ys_hbm.at[pl.ds(pos * ns, ns)], buf.at[slot, k, pl.ds(r * pitch, ns)],
                                     sem.at[slot])

    def issue(pref, slot):
        def body(r, c):
            for k in range(2):
                copy(k, pref[0, k * tc + r], slot, r).start(priority=k)
            return c

        lax.fori_loop(0, tc, body, 0, unroll=4)

    def consume(slot):
        def body(r, c):
            for k in range(2):
                copy(k, 0, slot, r).wait()
            return c

        lax.fori_loop(0, tc, body, 0, unroll=4)
        g0 = gt_ref[:, 0:1]
        g1 = gt_ref[:, 1:2]
        tiles = [None] * (2 * ns)
        for s in range(ns):
            w0 = buf[slot, 0, pl.ds(s, tc, stride=pitch), :]
            w1 = buf[slot, 1, pl.ds(s, tc, stride=pitch), :]
            lo = pltpu.bitcast(w0 << 16, F32) * g0 + pltpu.bitcast(w1 << 16, F32) * g1
            hi = (pltpu.bitcast(w0 & jnp.uint32(0xFFFF0000), F32) * g0
                  + pltpu.bitcast(w1 & jnp.uint32(0xFFFF0000), F32) * g1)
            tiles[2 * s] = lo
            tiles[2 * s + 1] = hi
        y = DEEPNORM_ALPHA * x_ref[...] + jnp.concatenate(tiles, axis=1)
        out = _layer_norm_rows(y, g_ref[...], b_ref[...])
        if head_blocks is None:
            o_ref[...] = out
            ob_ref[...] = out.astype(BF16)
        else:
            @pl.when(i < head_blocks)
            def _():
                o_ref[...] = out

            @pl.when(i >= head_blocks)
            def _():
                ob_ref[...] = out

    for slot in range(2):
        @pl.when(i % 2 == slot)
        def _(slot=slot):
            if slot == 0:
                @pl.when(i == 0)
                def _():
                    issue(pos_ref, 0)

            @pl.when(i + 1 < n)
            def _():
                issue(pos_next_ref, 1 - slot)

            consume(slot)


def _moe_combine_ln(x, ys, td, dest, gate, g, b, head_rows=None):
    T, D = x.shape
    ns = D // 2 // LANES
    pitch = ns + SUBLANES
    tc = _tile(T, 256, SUBLANES) if head_rows is None else _tile(math.gcd(T, head_rows), 256, SUBLANES)
    nblk = T // tc
    pos = dest.reshape(2, nblk, tc).transpose(1, 0, 2).reshape(nblk, 1, 2 * tc)
    row = pl.BlockSpec((tc, D), lambda i: (i, 0))
    vec = pl.BlockSpec((1, D), lambda i: (0, 0))
    if head_rows is None:
        head_blocks = None
        out_specs = [row, row]
        out_shape = [jax.ShapeDtypeStruct((T, D), F32), jax.ShapeDtypeStruct((T, D), BF16)]
    else:
        head_blocks = head_rows // tc
        out_specs = [pl.BlockSpec((tc, D), lambda i: (jnp.minimum(i, head_blocks - 1), 0)),
                     pl.BlockSpec((tc, D), lambda i: (jnp.maximum(i - head_blocks, 0), 0))]
        out_shape = [jax.ShapeDtypeStruct((head_rows, D), F32), jax.ShapeDtypeStruct((T - head_rows, D), F32)]
    vmem = 4 * tc * pitch * LANES * 4 + 2 * tc * D * 4 + 2 * tc * D * (4 + 4) + 6 * tc * D * 4 + 4 * 2**20
    return pl.pallas_call(
        functools.partial(_combine_ln_kernel, tc=tc, ns=ns, pitch=pitch, td=td, head_blocks=head_blocks),
        grid=(nblk,),
        in_specs=[pl.BlockSpec((None, 1, 2 * tc), lambda i: (i, 0, 0), memory_space=pltpu.SMEM),
                  pl.BlockSpec((None, 1, 2 * tc), lambda i: (jnp.minimum(i + 1, nblk - 1), 0, 0),
                               memory_space=pltpu.SMEM),
                  row, pl.BlockSpec((tc, 2), lambda i: (i, 0)), pl.BlockSpec(memory_space=pl.ANY), vec, vec],
        out_specs=out_specs,
        out_shape=out_shape,
        scratch_shapes=[pltpu.VMEM((2, 2, tc * pitch, LANES), jnp.uint32), pltpu.SemaphoreType.DMA((2,))],
        compiler_params=_params(("arbitrary",), vmem),
        name="moe_combine_ln",
    )(pos, pos, x, gate.T, ys, g.reshape(1, D), b.reshape(1, D))


def _moe_layout(idx, T):
    E = N_EXPERTS
    bm = MOE_BM
    nb = -(-2 * T // bm) + E
    rows = nb * bm
    flat_e = idx.reshape(-1)
    tok = jnp.tile(jnp.arange(T, dtype=jnp.int32), 2)
    onehot = (flat_e[:, None] == jnp.arange(E, dtype=jnp.int32)[None, :]).astype(jnp.int32)
    csum = jnp.cumsum(onehot, axis=0)
    rank = jnp.sum(onehot * csum, axis=1) - 1
    counts = csum[-1]
    padded = (counts + bm - 1) // bm * bm
    pad_end = jnp.cumsum(padded)
    pad_start = pad_end - padded
    dest = (pad_start[flat_e] + rank).astype(jnp.int32)
    buf_tok = jnp.zeros((rows,), jnp.int32).at[dest].set(tok)
    blk_start = jnp.arange(nb, dtype=jnp.int32) * bm
    blk_e = jnp.minimum(jnp.searchsorted(pad_end, blk_start, side="right"), E - 1).astype(jnp.int32)
    blk_valid = jnp.clip(pad_start[blk_e] + counts[blk_e] - blk_start, 0, bm).astype(jnp.int32)
    n_used_rows = pad_end[-1].astype(jnp.int32)
    return dest, buf_tok, blk_e, blk_valid, n_used_rows, rows


def _routed_experts_ln(x, xb, xp, router_w, router_b, w_gu, w_dn, layer, g, b, head_rows=None):
    T, D = x.shape
    idx, gate = _router(xb, router_w, router_b)
    dest, buf_tok, blk_e, blk_valid, n_used_rows, rows = _moe_layout(idx, T)
    xs = _moe_gather(xp, buf_tok, n_used_rows, rows, D)
    ys, td = _moe_experts(xs, w_gu, w_dn, layer, blk_e, blk_valid, n_used_rows // MOE_BM)
    return _moe_combine_ln(x, ys, td, dest, gate, g, b, head_rows)


def kernel(x_prompt, x_sample, mem_prompt, mem_sample, da_w_qkv, da_lam, da_subln_g, da_w_o, sg_w_in, sg_b_in, sg_ln_g, sg_ln_b, sg_w_s, sg_b_s, sg_w_o, rg_w_in, rg_conv_w, rg_conv_b, rg_w_a, rg_b_a, rg_w_i, rg_b_i, rg_lambda, rg_w_o, ca_w_q, ca_w_kv, ca_w_o, router_w, router_b, moe_w_gu, moe_w_dn, ln_g, ln_b):
    Bp, S, D = x_prompt.shape
    Bs = x_sample.shape[0]
    assert x_sample.shape[1] == S, "both request groups must share the sequence length"
    B = Bp + Bs
    M = mem_prompt.shape[1]
    x = jnp.concatenate([x_prompt.reshape(Bp * S, D), x_sample.reshape(Bs * S, D)], axis=0)
    xb = x.astype(BF16)
    mem = jnp.concatenate([mem_prompt.reshape(Bp * M, D), mem_sample.reshape(Bs * M, D)], axis=0).astype(BF16)
    slopes = jnp.asarray([2.0 ** (-8.0 * (h + 1) / DA_HEADS) for h in range(DA_HEADS)], F32)
    dh = D // (2 * DA_HEADS)
    qkv_scale = jnp.concatenate([jnp.full((D,), LOG2E * dh ** -0.5, F32), jnp.ones((2 * D,), F32)])

    for i in range(DEPTH):
        kind, slot = i % N_MIXERS, i // N_MIXERS
        if kind == 0:
            lam_init = 0.8 - 0.6 * math.exp(-0.3 * i)
            lf = da_lam[slot].astype(F32)
            lam_full = (jnp.exp(jnp.sum(lf[0] * lf[1])) - jnp.exp(jnp.sum(lf[2] * lf[3])) + lam_init).reshape(1)
            qkv = _mm(xb, da_w_qkv, slot, col_scale=qkv_scale, out_dtype=BF16, name="da_qkv")
            o = _diff_attention(qkv, B, S, slopes, lam_full, da_subln_g[slot], lam_init)
            h = _mm(o, da_w_o, slot, out_dtype=BF16, name="da_out")
        elif kind == 1:
            z = _mm(xb, sg_w_in.astype(BF16), slot, bias=sg_b_in[slot], act="gelu", out_dtype=BF16, name="sg_in")
            gated = _spatial_gate(z, sg_ln_g[slot], sg_ln_b[slot], sg_w_s[slot], sg_b_s[slot])
            h = _mm(gated, sg_w_o.astype(BF16), slot, out_dtype=BF16, name="sg_out")
        else:
            xg = _mm(xb, rg_w_in, slot, name="rg_in")
            hr = _rglru_core(xg, B, S, rg_conv_w[slot], rg_conv_b[slot], rg_w_a[slot], rg_b_a[slot],
                             rg_w_i[slot], rg_b_i[slot], rg_lambda[slot])
            h = _mm(hr, rg_w_o, slot, out_dtype=BF16, name="rg_out")
        x, xb = _add_ln(x, h, ln_g[i, 0], ln_b[i, 0])

        q = _mm(xb, ca_w_q, i, out_dtype=BF16, name="ca_q")
        kv = _mm(mem, ca_w_kv, i, out_dtype=BF16, name="ca_kv")
        o = _memory_attention_core(q, kv, B, S, M)
        h = _mm(o, ca_w_o, i, out_dtype=BF16, name="ca_out")
        x, xb, xp = _add_ln(x, h, ln_g[i, 1], ln_b[i, 1], emit_packed=True)

        x, xb = _routed_experts_ln(x, xb, xp, router_w, router_b, moe_w_gu, moe_w_dn, i, ln_g[i, 2], ln_b[i, 2],
                                   head_rows=Bp * S if i == DEPTH - 1 else None)

    return (x.reshape(Bp, S, D), xb.reshape(Bs, S, D))
```
